```python
import jax
import jax.numpy as jnp
from jax import lax
import numpy as np

D_MODEL = 1024
BATCH = 16
SEQ = 256
DEPTH = 4
DEC_BATCH = 2
DEC_SEQ = 4096
PAST_LEN = 256

GRID_W = 64
N_EVEN = (DEPTH + 1) // 2
N_ODD = DEPTH // 2
EPS = 1e-6
ROPE_BASE = 10000.0
Q_BLOCK = 128
F32 = jnp.float32

MLA_HEADS = 8
MLA_NOPE = 64
MLA_ROPE = 32
MLA_QK = MLA_NOPE + MLA_ROPE
MLA_V = 64
Q_LORA = 384
KV_LORA = 256

GDN_HEADS = 8
GDN_DK = 64
GDN_DV = 64
GDN_CONV = 3
GDN_CHUNK = 64

MLSTM_HEADS = 8
MLSTM_DK = 64
MLSTM_DV = 128
MLSTM_CHUNK = 64

D_FF = -(-8 * D_MODEL // (3 * 256)) * 256

EVEN_SIZES = (Q_LORA, KV_LORA, MLA_ROPE, GDN_HEADS * GDN_DK, GDN_HEADS * GDN_DK,
              GDN_HEADS * GDN_DV, GDN_HEADS * GDN_DV, 2 * GDN_HEADS, 2 * GDN_HEADS)
EVEN_IN = sum(EVEN_SIZES)
EVEN_MIX = MLA_HEADS * MLA_V + GDN_HEADS * GDN_DV
GDN_CONV_CH = GDN_HEADS * (2 * GDN_DK + GDN_DV)
ODD_SIZES = (MLSTM_HEADS * MLSTM_DK, MLSTM_HEADS * MLSTM_DK, MLSTM_HEADS * MLSTM_DV,
             MLSTM_HEADS * MLSTM_DV, 4 * MLSTM_HEADS)
ODD_IN = sum(ODD_SIZES)
ODD_MIX = MLSTM_HEADS * MLSTM_DV

kernel_name = 'hybrid_mla_gdn_mlstm_diffusion_step'


def split_last(x, sizes):
    return jnp.split(x, [int(i) for i in np.cumsum(sizes)[:-1]], axis=-1)


def rms_norm(x, w):
    xf = x.astype(F32)
    y = xf * lax.rsqrt(jnp.mean(xf * xf, axis=-1, keepdims=True) + EPS)
    return (y * w.astype(F32)).astype(x.dtype)


def l2_norm(x):
    xf = x.astype(F32)
    return (xf * lax.rsqrt(jnp.sum(xf * xf, axis=-1, keepdims=True) + EPS)).astype(x.dtype)


def flip_t(a):
    return jnp.flip(a, axis=1)


def axial_rope_table(n_tokens, dtype):
    rows = n_tokens // GRID_W
    row = jnp.repeat(jnp.arange(rows, dtype=F32), GRID_W)
    col = jnp.tile(jnp.arange(GRID_W, dtype=F32), rows)
    per_axis = MLA_ROPE // 2
    inv = 1.0 / (ROPE_BASE ** (jnp.arange(0, per_axis, 2, dtype=F32) / per_axis))
    ar = row[:, None] * inv
    ac = col[:, None] * inv
    ang = jnp.concatenate([ar, ar, ac, ac], axis=-1)
    return jnp.cos(ang).astype(dtype), jnp.sin(ang).astype(dtype)


def apply_axial_rope(x, cos, sin):
    a = x.reshape(*x.shape[:-1], 2, 2, MLA_ROPE // 4)
    rot = jnp.concatenate([-a[..., 1:, :], a[..., :1, :]], axis=-2).reshape(x.shape)
    return x * cos[:, None, :] + rot * sin[:, None, :]


def rope_tail(x, cos, sin):
    return jnp.concatenate([x[..., :MLA_NOPE], apply_axial_rope(x[..., MLA_NOPE:], cos, sin)], axis=-1)


def adaln(cond, w, b):
    mod = jax.nn.silu(cond) @ w + b
    return jnp.split(mod[:, None, :], 6, axis=-1)


def swiglu(h, w_in, w_out):
    gate, up = jnp.split(h @ w_in, 2, axis=-1)
    return (jax.nn.silu(gate) * up) @ w_out


def short_conv(x, w):
    ch = x.shape[-1]
    return lax.conv_general_dilated(x, w[:, None, :].astype(x.dtype), window_strides=(1,),
                                    padding=[(GDN_CONV // 2, GDN_CONV // 2)],
                                    dimension_numbers=('NWC', 'WIO', 'NWC'), feature_group_count=ch)


def block_attention(q, k, v):
    B, Tq, H, Dk = q.shape
    Dv = v.shape[-1]
    nb = Tq // Q_BLOCK
    scale = Dk ** -0.5
    qb = jnp.moveaxis(q.reshape(B, nb, Q_BLOCK, H, Dk), 1, 0)

    def one_block(q_blk):
        s = jnp.einsum('bqhd,bkhd->bhqk', q_blk, k).astype(F32) * scale
        p = jax.nn.softmax(s, axis=-1).astype(v.dtype)
        return jnp.einsum('bhqk,bkhd->bqhd', p, v)

    o = lax.map(one_block, qb)
    return jnp.moveaxis(o, 0, 1).reshape(B, Tq, H, Dv)


def mla_expand(ckv, krope, w_ukv, k_norm):
    B, T, _ = ckv.shape
    kv = (ckv @ w_ukv).reshape(B, T, MLA_HEADS, MLA_NOPE + MLA_V)
    k_nope, v = kv[..., :MLA_NOPE], kv[..., MLA_NOPE:]
    k = jnp.concatenate([k_nope, jnp.broadcast_to(krope[:, :, None, :], (B, T, MLA_HEADS, MLA_ROPE))], axis=-1)
    return rms_norm(k, k_norm), v


def chunk_major(a, n_chunks, length):
    B, _, H = a.shape[:3]
    a = a.astype(F32).reshape(B, n_chunks, length, H, *a.shape[3:])
    return jnp.moveaxis(jnp.moveaxis(a, 3, 2), 1, 0)


def gated_delta_scan(q, k, v, g, beta, s0):
    B, T, H, DK = q.shape
    DV = v.shape[-1]
    L = GDN_CHUNK
    nc = T // L
    qc, kc, vc, gc_in, bc = (chunk_major(a, nc, L) for a in (q, k, v, g, beta))
    qc = qc * (DK ** -0.5)
    gc = jnp.cumsum(gc_in, axis=-1)
    tri = jnp.tril(jnp.ones((L, L), dtype=bool))
    strict = jnp.tril(jnp.ones((L, L), dtype=bool), -1)
    decay = jnp.exp(jnp.where(tri, gc[..., :, None] - gc[..., None, :], -jnp.inf))
    kb = kc * bc[..., None]
    a_mat = jnp.where(strict, jnp.einsum('cbhid,cbhjd->cbhij', kb, kc) * decay, 0.0)
    eye = jnp.eye(L, dtype=F32)
    t_mat = lax.linalg.triangular_solve(a_mat + eye, jnp.broadcast_to(eye, a_mat.shape),
                                        left_side=True, lower=True)
    u = jnp.einsum('cbhij,cbhjv->cbhiv', t_mat, vc * bc[..., None])
    w = jnp.einsum('cbhij,cbhjd->cbhid', t_mat, kb * jnp.exp(gc)[..., None])
    attn = jnp.einsum('cbhid,cbhjd->cbhij', qc, kc) * decay
    q_dec = qc * jnp.exp(gc)[..., None]
    k_dec = kc * jnp.exp(gc[..., -1:] - gc)[..., None]
    g_last = jnp.exp(gc[..., -1])

    def step(s, inp):
        u_i, w_i, a_i, qd_i, kd_i, gl_i = inp
        v_new = u_i - jnp.einsum('bhld,bhdv->bhlv', w_i, s)
        o = jnp.einsum('bhld,bhdv->bhlv', qd_i, s) + jnp.einsum('bhls,bhsv->bhlv', a_i, v_new)
        s = gl_i[..., None, None] * s + jnp.einsum('bhld,bhlv->bhdv', kd_i, v_new)
        return s, o

    s, o = lax.scan(step, s0.astype(F32), (u, w, attn, q_dec, k_dec, g_last))
    o = jnp.moveaxis(jnp.moveaxis(o, 0, 1), 2, 3).reshape(B, T, H, DV)
    return o.astype(v.dtype), s


def mlstm_scan(q, k, v, log_i, log_f, c0, n0, m0):
    B, T, H, DK = q.shape
    DV = v.shape[-1]
    L = MLSTM_CHUNK
    nc = T // L
    qc, kc, vc, ic, fc = (chunk_major(a, nc, L) for a in (q, k, v, log_i, log_f))
    tri = jnp.tril(jnp.ones((L, L), dtype=bool))
    b = jnp.cumsum(fc, axis=-1)
    d_log = jnp.where(tri, b[..., :, None] - b[..., None, :] + ic[..., None, :], -jnp.inf)
    d_max = jnp.max(d_log, axis=-1)
    qk = jnp.einsum('cbhld,cbhsd->cbhls', qc, kc)
    g_log = b[..., -1:] - b + ic
    g_max = jnp.max(g_log, axis=-1)

    def step(carry, inp):
        cm, nv, m = carry
        q_i, k_i, v_i, b_i, dl_i, dm_i, qk_i, gl_i, gm_i = inp
        inter = b_i + m[..., None]
        m_t = jnp.maximum(inter, dm_i)
        wts = jnp.exp(dl_i - m_t[..., None]) * qk_i
        a = jnp.exp(inter - m_t)
        num = jnp.einsum('bhls,bhsv->bhlv', wts, v_i) + a[..., None] * jnp.einsum('bhld,bhdv->bhlv', q_i, cm)
        den = jnp.sum(wts, axis=-1) + a * jnp.einsum('bhld,bhd->bhl', q_i, nv)
        h = num / jnp.maximum(jnp.abs(den), jnp.exp(-m_t))[..., None]
        b_last = b_i[..., -1]
        m_new = jnp.maximum(b_last + m, gm_i)
        ws = jnp.exp(gl_i - m_new[..., None])
        dec = jnp.exp(b_last + m - m_new)
        cm = dec[..., None, None] * cm + jnp.einsum('bhs,bhsd,bhsv->bhdv', ws, k_i, v_i)
        nv = dec[..., None] * nv + jnp.einsum('bhs,bhsd->bhd', ws, k_i)
        return (cm, nv, m_new), h

    init = (c0.astype(F32), n0.astype(F32), m0.astype(F32))
    (cm, nv, m), h = lax.scan(step, init, (qc, kc, vc, b, d_log, d_max, qk, g_log, g_max))
    h = jnp.moveaxis(jnp.moveaxis(h, 0, 1), 2, 3).reshape(B, T, H, DV)
    return h.astype(v.dtype), (cm, nv, m)


def even_mixer(h, w_in, w_out, q_a_norm, kv_a_norm, w_uq, w_ukv, q_norm, k_norm,
               conv_w, a_log, dt_bias, out_norm, rope, ctx):
    B, T, _ = h.shape
    cq, ckv, krope, gq, gk, gv, gz, ga, gb = split_last(h @ w_in, EVEN_SIZES)
    ckv = rms_norm(ckv, kv_a_norm)
    q = rms_norm((rms_norm(cq, q_a_norm) @ w_uq).reshape(B, T, MLA_HEADS, MLA_QK), q_norm)
    k, v = mla_expand(ckv, krope, w_ukv, k_norm)
    if rope is not None:
        cos, sin = rope
        q = rope_tail(q, cos, sin)
        k = rope_tail(k, cos, sin)
    if ctx is not None:
        ckv_ctx, krope_ctx, s0 = ctx
        k_ctx, v_ctx = mla_expand(ckv_ctx, krope_ctx, w_ukv, k_norm)
        k = jnp.concatenate([k_ctx, k], axis=1)
        v = jnp.concatenate([v_ctx, v], axis=1)
    else:
        s0 = jnp.zeros((B, 2, GDN_HEADS, GDN_DK, GDN_DV), F32)
    attn_out = block_attention(q, k, v).reshape(B, T, MLA_HEADS * MLA_V)
    qkv = jax.nn.silu(short_conv(jnp.concatenate([gq, gk, gv], axis=-1), conv_w))
    gq, gk, gv = split_last(qkv, (GDN_HEADS * GDN_DK, GDN_HEADS * GDN_DK, GDN_HEADS * GDN_DV))
    gq = l2_norm(gq.reshape(B, T, GDN_HEADS, GDN_DK))
    gk = l2_norm(gk.reshape(B, T, GDN_HEADS, GDN_DK))
    gv = gv.reshape(B, T, GDN_HEADS, GDN_DV)
    log_decay = -jnp.exp(a_log.astype(F32)) * jax.nn.softplus(
        ga.reshape(B, T, 2, GDN_HEADS).astype(F32) + dt_bias.astype(F32))
    beta = jax.nn.sigmoid(gb.reshape(B, T, 2, GDN_HEADS).astype(F32))
    o_f, s_f = gated_delta_scan(gq, gk, gv, log_decay[:, :, 0], beta[:, :, 0], s0[:, 0])
    o_b, s_b = gated_delta_scan(flip_t(gq), flip_t(gk), flip_t(gv), flip_t(log_decay[:, :, 1]),
                                flip_t(beta[:, :, 1]), s0[:, 1])
    delta_out = rms_norm(o_f + flip_t(o_b), out_norm) * jax.nn.silu(gz.reshape(B, T, GDN_HEADS, GDN_DV))
    out = jnp.concatenate([attn_out, delta_out.reshape(B, T, GDN_HEADS * GDN_DV)], axis=-1) @ w_out
    if ctx is None:
        return out, (ckv, krope, jnp.stack([s_f, s_b], axis=1).astype(h.dtype))
    return out, None


def odd_mixer(h, w_in, w_out, gate_bias, out_norm, ctx):
    B, T, _ = h.shape
    q, k, v, o, gates = split_last(h @ w_in, ODD_SIZES)
    q = q.reshape(B, T, MLSTM_HEADS, MLSTM_DK)
    k = k.reshape(B, T, MLSTM_HEADS, MLSTM_DK) * (MLSTM_DK ** -0.5)
    v = v.reshape(B, T, MLSTM_HEADS, MLSTM_DV)
    gates = gates.reshape(B, T, 4, MLSTM_HEADS).astype(F32) + gate_bias.astype(F32)
    log_i = gates[:, :, 0:2]
    log_f = jax.nn.log_sigmoid(gates[:, :, 2:4])
    if ctx is None:
        c0 = jnp.zeros((B, 2, MLSTM_HEADS, MLSTM_DK, MLSTM_DV), F32)
        n0 = jnp.zeros((B, 2, MLSTM_HEADS, MLSTM_DK), F32)
        m0 = jnp.zeros((B, 2, MLSTM_HEADS), F32)
    else:
        c0, n0, m0 = ctx
    h_f, st_f = mlstm_scan(q, k, v, log_i[:, :, 0], log_f[:, :, 0], c0[:, 0], n0[:, 0], m0[:, 0])
    h_b, st_b = mlstm_scan(flip_t(q), flip_t(k), flip_t(v), flip_t(log_i[:, :, 1]), flip_t(log_f[:, :, 1]),
                           c0[:, 1], n0[:, 1], m0[:, 1])
    mem = rms_norm(h_f + flip_t(h_b), out_norm.reshape(MLSTM_HEADS, MLSTM_DV))
    out = (mem * jax.nn.sigmoid(o.reshape(B, T, MLSTM_HEADS, MLSTM_DV))).reshape(B, T, ODD_MIX) @ w_out
    if ctx is None:
        new_c = jnp.stack([st_f[0], st_b[0]], axis=1).astype(h.dtype)
        new_n = jnp.stack([st_f[1], st_b[1]], axis=1).astype(h.dtype)
        new_m = jnp.stack([st_f[2], st_b[2]], axis=1).astype(h.dtype)
        return out, (new_c, new_n, new_m)
    return out, None


def setup_inputs(seed: int = 0) -> dict:
    key = jax.random.key(seed)
    it = iter(jax.random.split(key, 64))

    def nrm(shape, scale):
        return jax.random.normal(next(it), shape, F32) * scale

    def gain(shape):
        return 1.0 + nrm(shape, 0.02)

    a_log = jnp.log(jax.random.uniform(next(it), (N_EVEN, 2, GDN_HEADS), F32, 1.0, 16.0))
    dt = jnp.exp(jax.random.uniform(next(it), (N_EVEN, 2, GDN_HEADS), F32, float(np.log(1e-3)), float(np.log(1e-1))))
    dt_bias = dt + jnp.log(-jnp.expm1(-dt))
    i_bias = nrm((N_ODD, 2, MLSTM_HEADS), 0.1)
    f_bias = 3.0 + jax.random.uniform(next(it), (N_ODD, 2, MLSTM_HEADS), F32, 0.0, 3.0)
    return {
        'x_prompt': nrm((BATCH, SEQ, D_MODEL), 1.0),
        'x_sample': nrm((DEC_BATCH, DEC_SEQ, D_MODEL), 1.0),
        'c': nrm((DEC_BATCH, D_MODEL), 1.0),
        'c_ctx': nrm((D_MODEL,), 1.0),
        'cache_mla_ckv': nrm((DEC_BATCH, N_EVEN, PAST_LEN, KV_LORA), 1.0),
        'cache_mla_krope': nrm((DEC_BATCH, N_EVEN, PAST_LEN, MLA_ROPE), 1.0),
        'state_gdn': nrm((DEC_BATCH, N_EVEN, 2, GDN_HEADS, GDN_DK, GDN_DV), 0.3),
        'state_mlstm_C': nrm((DEC_BATCH, N_ODD, 2, MLSTM_HEADS, MLSTM_DK, MLSTM_DV), 0.3),
        'state_mlstm_n': nrm((DEC_BATCH, N_ODD, 2, MLSTM_HEADS, MLSTM_DK), 0.3),
        'state_mlstm_m': nrm((DEC_BATCH, N_ODD, 2, MLSTM_HEADS), 0.5),
        'norm_mix': gain((DEPTH, D_MODEL)),
        'norm_ffn': gain((DEPTH, D_MODEL)),
        'w_ada': nrm((DEPTH, D_MODEL, 6 * D_MODEL), 0.5 * D_MODEL ** -0.5),
        'b_ada': nrm((DEPTH, 6 * D_MODEL), 0.02),
        'w_ffn_in': nrm((DEPTH, D_MODEL, 2 * D_FF), D_MODEL ** -0.5),
        'w_ffn_out': nrm((DEPTH, D_FF, D_MODEL), D_FF ** -0.5),
        'w_even_in': nrm((N_EVEN, D_MODEL, EVEN_IN), D_MODEL ** -0.5),
        'w_even_out': nrm((N_EVEN, EVEN_MIX, D_MODEL), EVEN_MIX ** -0.5),
        'mla_q_a_norm': gain((N_EVEN, Q_LORA)),
        'mla_kv_a_norm': gain((N_EVEN, KV_LORA)),
        'w_mla_uq': nrm((N_EVEN, Q_LORA, MLA_HEADS * MLA_QK), Q_LORA ** -0.5),
        'w_mla_ukv': nrm((N_EVEN, KV_LORA, MLA_HEADS * (MLA_NOPE + MLA_V)), KV_LORA ** -0.5),
        'mla_q_norm': gain((N_EVEN, MLA_QK)),
        'mla_k_norm': gain((N_EVEN, MLA_QK)),
        'gdn_conv': nrm((N_EVEN, GDN_CONV, GDN_CONV_CH), GDN_CONV ** -0.5),
        'gdn_a_log': a_log,
        'gdn_dt_bias': dt_bias,
        'gdn_out_norm': gain((N_EVEN, GDN_DV)),
        'w_odd_in': nrm((N_ODD, D_MODEL, ODD_IN), D_MODEL ** -0.5),
        'w_odd_out': nrm((N_ODD, ODD_MIX, D_MODEL), ODD_MIX ** -0.5),
        'mlstm_gate_bias': jnp.concatenate([i_bias, f_bias], axis=1),
        'mlstm_out_norm': gain((N_ODD, MLSTM_HEADS * MLSTM_DV)),
    }


def reference(x_prompt, x_sample, c, c_ctx,
              cache_mla_ckv, cache_mla_krope, state_gdn, state_mlstm_C, state_mlstm_n, state_mlstm_m,
              norm_mix, norm_ffn, w_ada, b_ada, w_ffn_in, w_ffn_out,
              w_even_in, w_even_out, mla_q_a_norm, mla_kv_a_norm, w_mla_uq, w_mla_ukv,
              mla_q_norm, mla_k_norm, gdn_conv, gdn_a_log, gdn_dt_bias, gdn_out_norm,
              w_odd_in, w_odd_out, mlstm_gate_bias, mlstm_out_norm):
    rope_lat = axial_rope_table(x_sample.shape[1], x_sample.dtype)

    def layer(x, cond, l, rope, ctx):
        j = l // 2
        sh1, sc1, g1, sh2, sc2, g2 = adaln(cond, w_ada[l], b_ada[l])
        h = rms_norm(x, norm_mix[l]) * (1.0 + sc1) + sh1
        if l % 2 == 0:
            out, ctx_new = even_mixer(h, w_even_in[j], w_even_out[j], mla_q_a_norm[j], mla_kv_a_norm[j],
                                      w_mla_uq[j], w_mla_ukv[j], mla_q_norm[j], mla_k_norm[j],
                                      gdn_conv[j], gdn_a_log[j], gdn_dt_bias[j], gdn_out_norm[j], rope, ctx)
        else:
            out, ctx_new = odd_mixer(h, w_odd_in[j], w_odd_out[j], mlstm_gate_bias[j], mlstm_out_norm[j], ctx)
        x = x + g1 * out
        h = rms_norm(x, norm_ffn[l]) * (1.0 + sc2) + sh2
        x = x + g2 * swiglu(h, w_ffn_in[l], w_ffn_out[l])
        return x, ctx_new

    xp, xs = x_prompt, x_sample
    cond_ctx = c_ctx[None, :]
    ckv_new, krope_new, gdn_new, mc_new, mn_new, mm_new = [], [], [], [], [], []
    for l in range(DEPTH):
        j = l // 2
        xp, ctx_p = layer(xp, cond_ctx, l, None, None)
        if l % 2 == 0:
            ckv_new.append(ctx_p[0])
            krope_new.append(ctx_p[1])
            gdn_new.append(ctx_p[2])
            ctx_s = (cache_mla_ckv[:, j], cache_mla_krope[:, j], state_gdn[:, j])
        else:
            mc_new.append(ctx_p[0])
            mn_new.append(ctx_p[1])
            mm_new.append(ctx_p[2])
            ctx_s = (state_mlstm_C[:, j], state_mlstm_n[:, j], state_mlstm_m[:, j])
        xs, _ = layer(xs, c, l, rope_lat, ctx_s)
    return (xp, xs, jnp.stack(ckv_new, axis=1), jnp.stack(krope_new, axis=1), jnp.stack(gdn_new, axis=1),
            jnp.stack(mc_new, axis=1), jnp.stack(mn_new, axis=1), jnp.stack(mm_new, axis=1))
```

```python
import functools

import jax
import jax.numpy as jnp
import numpy as np
from jax import lax
from jax.experimental import pallas as pl
from jax.experimental.pallas import tpu as pltpu

F32 = jnp.float32
BF16 = jnp.bfloat16
HIGHEST = lax.Precision.HIGHEST

D_MODEL = 1024
BATCH = 16
SEQ = 256
DEPTH = 4
DEC_BATCH = 2
DEC_SEQ = 4096
PAST_LEN = 256
GRID_W = 64
EPS = 1e-6
ROPE_BASE = 10000.0
MLA_HEADS = 8
MLA_NOPE = 64
MLA_ROPE = 32
MLA_QK = MLA_NOPE + MLA_ROPE
MLA_V = 64
Q_LORA = 384
KV_LORA = 256
GDN_HEADS = 8
GDN_DK = 64
GDN_DV = 64
MLSTM_HEADS = 8
MLSTM_DK = 64
MLSTM_DV = 128
D_FF = 2816

LANES = 128
HALF = 64
CH = 64
NP_ROWS = BATCH * SEQ
NS_ROWS = DEC_BATCH * DEC_SEQ
T_ALL = NP_ROWS + NS_ROWS
GROUP = 4096
assert NP_ROWS == GROUP and DEC_SEQ == GROUP

TM_IN = 256
TM_FFN = 512
TF = 256
TQ = 256
SCAN_ROWS = 256
VMEM_LIMIT = 56 * 1024 * 1024

EVEN_W = 384 + 256 + 128 + 1536 + 512 + 128
ODD_W = 512 + 512 + 1024 + 1024 + 128


def _cparams(*sem):
    return pltpu.CompilerParams(dimension_semantics=sem, vmem_limit_bytes=VMEM_LIMIT)


def _silu(x):
    return x * jax.nn.sigmoid(x)


def _softplus(x):
    return jnp.maximum(x, 0.0) + jnp.log(1.0 + jnp.exp(-jnp.abs(x)))


def _dot(a, b):
    return jnp.dot(a.astype(BF16), b.astype(BF16), preferred_element_type=F32)


def _dot_hi(a, b):
    return jnp.dot(a, b, precision=HIGHEST, preferred_element_type=F32)


def _norm_mod(x, nw, sc, sh):
    ms = jnp.mean(x * x, axis=-1, keepdims=True)
    return (x * lax.rsqrt(ms + EPS) * nw) * (1.0 + sc) + sh


def _ada_kernel(c_ref, w_ref, b_ref, o_ref):
    c = c_ref[...]
    o_ref[...] = _dot(_silu(c), w_ref[...]) + b_ref[...]


def _ada_call(cond8, w_ada, b_ada):
    tn = 1536
    n = w_ada.shape[-1]
    return pl.pallas_call(
        _ada_kernel,
        grid=(DEPTH, n // tn),
        in_specs=[pl.BlockSpec((8, D_MODEL), lambda l, j: (0, 0)),
                  pl.BlockSpec((None, D_MODEL, tn), lambda l, j: (l, 0, j)),
                  pl.BlockSpec((None, 1, tn), lambda l, j: (l, 0, j))],
        out_specs=pl.BlockSpec((None, 8, tn), lambda l, j: (l, 0, j)),
        out_shape=jax.ShapeDtypeStruct((DEPTH, 8, n), F32),
        compiler_params=_cparams("parallel", "parallel"),
        name="ada",
    )(cond8, w_ada, b_ada.reshape(DEPTH, 1, n))


def _rope(x, cos, sa, sb):
    return x * cos + pltpu.roll(x, LANES - 8, 1) * sa + pltpu.roll(x, 8, 1) * sb


def _head_norm_rope(x, w, rope, scale):
    ms = jnp.sum(x * x, axis=-1, keepdims=True) * (1.0 / MLA_QK)
    y = x * lax.rsqrt(ms + EPS) * w
    if rope is not None:
        y = _rope(y, *rope)
    if scale != 1.0:
        y = y * scale
    return y


def _mla_kv(ckvn, kr, wk_ref, wv_ref, kn, rope, k_ref, v_ref):
    cb = ckvn.astype(BF16)
    kraw = jnp.dot(cb, wk_ref[...], preferred_element_type=F32)
    for h in range(MLA_HEADS):
        sl = slice(h * LANES, (h + 1) * LANES)
        k_ref[:, sl] = _head_norm_rope(kraw[:, sl] + kr, kn, rope, 1.0).astype(BF16)
    v_ref[...] = jnp.dot(cb, wv_ref[...], preferred_element_type=F32).astype(BF16)


def _even_in_kernel(x_ref, m_ref, nw_ref, w_ref, qan_ref, kvan_ref, wuq_ref, wk_ref, wv_ref, qn_ref, kn_ref,
                    cos_ref, sa_ref, sb_ref, gp_ref,
                    q_ref, k_ref, v_ref, ckv_ref, kr_ref, gqkv_ref, gz_ref, gate_ref):
    x = x_ref[...]
    h = _norm_mod(x, nw_ref[...], m_ref[1:2, :], m_ref[0:1, :])
    proj = jnp.dot(h.astype(BF16), w_ref[...], preferred_element_type=F32)
    rope = (cos_ref[...], sa_ref[...], sb_ref[...])
    cq = proj[:, 0:384]
    cqn = cq * lax.rsqrt(jnp.mean(cq * cq, axis=-1, keepdims=True) + EPS) * qan_ref[...]
    qraw = jnp.dot(cqn.astype(BF16), wuq_ref[...], preferred_element_type=F32)
    qn = qn_ref[...]
    for hd in range(MLA_HEADS):
        sl = slice(hd * LANES, (hd + 1) * LANES)
        q_ref[:, sl] = _head_norm_rope(qraw[:, sl], qn, rope, MLA_QK ** -0.5).astype(BF16)
    ckv = proj[:, 384:640]
    ckvn = ckv * lax.rsqrt(jnp.mean(ckv * ckv, axis=-1, keepdims=True) + EPS) * kvan_ref[...]
    ckv_ref[...] = ckvn
    kr = proj[:, 640:768]
    kr_ref[...] = kr
    _mla_kv(ckvn, kr, wk_ref, wv_ref, kn_ref[...], rope, k_ref, v_ref)
    gqkv_ref[...] = proj[:, 768:2304]
    gz_ref[...] = proj[:, 2304:2816]
    g = proj[:, 2816:2944]
    lane = lax.broadcasted_iota(jnp.int32, g.shape, 1)
    log_decay = -jnp.exp(gp_ref[0:1, :]) * _softplus(g + gp_ref[1:2, :])
    gate_ref[...] = jnp.where(lane < 16, log_decay, jax.nn.sigmoid(g))


def _full(shape):
    nd = len(shape)
    return pl.BlockSpec(shape, lambda *_: (0,) * nd)


def _even_in_call(x, mod3, nw, w, qan, kvan, wuq, wk, wv, qn, kn, cos, sa, sb, gp):
    tm = TM_IN
    nt = GROUP // tm
    row = lambda w_: pl.BlockSpec((tm, w_), lambda i: (i, 0))
    rope_spec = pl.BlockSpec((tm, LANES), lambda i: (jnp.where(i < nt, i, nt + i % nt), 0))
    outs = [(1024, BF16), (1024, BF16), (512, BF16), (256, F32), (128, F32), (1536, F32), (512, F32), (128, F32)]
    return pl.pallas_call(
        _even_in_kernel,
        grid=(T_ALL // tm,),
        in_specs=[row(D_MODEL),
                  pl.BlockSpec((None, 8, D_MODEL), lambda i: (i // nt, 0, 0)),
                  _full((1, D_MODEL)), _full(w.shape), _full((1, 384)), _full((1, 256)),
                  _full(wuq.shape), _full(wk.shape), _full(wv.shape), _full((1, LANES)), _full((1, LANES)),
                  rope_spec, rope_spec, rope_spec, _full((8, LANES))],
        out_specs=[row(w_) for w_, _ in outs],
        out_shape=[jax.ShapeDtypeStruct((T_ALL, w_), dt) for w_, dt in outs],
        compiler_params=_cparams("parallel"),
        name="even_in",
    )(x, mod3, nw, w, qan, kvan, wuq, wk, wv, qn, kn, cos, sa, sb, gp)


def _ctx_kv_kernel(ckv_ref, kr_ref, wk_ref, wv_ref, kn_ref, k_ref, v_ref):
    _mla_kv(ckv_ref[...], kr_ref[...], wk_ref, wv_ref, kn_ref[...], None, k_ref, v_ref)


def _ctx_kv_call(ckv, kr128, wk, wv, kn):
    n = ckv.shape[0]
    return pl.pallas_call(
        _ctx_kv_kernel,
        grid=(1,),
        in_specs=[_full(ckv.shape), _full(kr128.shape), _full(wk.shape), _full(wv.shape), _full((1, LANES))],
        out_specs=[_full((n, 1024)), _full((n, 512))],
        out_shape=[jax.ShapeDtypeStruct((n, 1024), BF16), jax.ShapeDtypeStruct((n, 512), BF16)],
        compiler_params=_cparams("arbitrary"),
        name="ctx_kv",
    )(ckv, kr128, wk, wv, kn)


def _attn_kernel(*refs, has_ctx):
    if has_ctx:
        q_ref, k_ref, v_ref, kc_ref, vc_ref, o_ref = refs
    else:
        q_ref, k_ref, v_ref, o_ref = refs
    nt = (((1,), (1,)), ((), ()))
    for p in range(MLA_HEADS // 2):
        psl = slice(p * LANES, (p + 1) * LANES)
        v2 = v_ref[:, psl]
        lane = lax.broadcasted_iota(jnp.int32, v2.shape, 1)
        vs = (jnp.where(lane < HALF, v2, jnp.zeros_like(v2)), jnp.where(lane >= HALF, v2, jnp.zeros_like(v2)))
        if has_ctx:
            vc2 = vc_ref[:, psl]
            lc = lax.broadcasted_iota(jnp.int32, vc2.shape, 1)
            vcs = (jnp.where(lc < HALF, vc2, jnp.zeros_like(vc2)), jnp.where(lc >= HALF, vc2, jnp.zeros_like(vc2)))
        acc = None
        for hh in range(2):
            sl = slice((2 * p + hh) * LANES, (2 * p + hh + 1) * LANES)
            qh = q_ref[:, sl]
            s = lax.dot_general(qh, k_ref[:, sl], nt, preferred_element_type=F32)
            m = jnp.max(s, axis=-1, keepdims=True)
            if has_ctx:
                sc = lax.dot_general(qh, kc_ref[:, sl], nt, preferred_element_type=F32)
                m = jnp.maximum(m, jnp.max(sc, axis=-1, keepdims=True))
            e = jnp.exp(s - m)
            l = jnp.sum(e, axis=-1, keepdims=True)
            o = jnp.dot(e.astype(BF16), vs[hh], preferred_element_type=F32)
            if has_ctx:
                ec = jnp.exp(sc - m)
                l = l + jnp.sum(ec, axis=-1, keepdims=True)
                o = o + jnp.dot(ec.astype(BF16), vcs[hh], preferred_element_type=F32)
            o = o / l
            acc = o if acc is None else acc + o
        o_ref[:, psl] = acc.astype(BF16)


def _attn_prompt_call(q, k, v):
    blk = lambda w_: pl.BlockSpec((SEQ, w_), lambda b: (b, 0))
    return pl.pallas_call(
        functools.partial(_attn_kernel, has_ctx=False),
        grid=(BATCH,),
        in_specs=[blk(1024), blk(1024), blk(512)],
        out_specs=blk(512),
        out_shape=jax.ShapeDtypeStruct((NP_ROWS, 512), BF16),
        compiler_params=_cparams("parallel"),
        name="attn_prompt",
    )(q, k, v)


def _attn_sample_call(q, k, v, kc, vc):
    nq = DEC_SEQ // TQ
    qoff = NP_ROWS // TQ
    return pl.pallas_call(
        functools.partial(_attn_kernel, has_ctx=True),
        grid=(DEC_BATCH, nq),
        in_specs=[pl.BlockSpec((TQ, 1024), lambda b, i: (qoff + b * nq + i, 0)),
                  pl.BlockSpec((DEC_SEQ, 1024), lambda b, i: (1 + b, 0)),
                  pl.BlockSpec((DEC_SEQ, 512), lambda b, i: (1 + b, 0)),
                  pl.BlockSpec((PAST_LEN, 1024), lambda b, i: (b, 0)),
                  pl.BlockSpec((PAST_LEN, 512), lambda b, i: (b, 0))],
        out_specs=pl.BlockSpec((TQ, 512), lambda b, i: (b * nq + i, 0)),
        out_shape=jax.ShapeDtypeStruct((NS_ROWS, 512), BF16),
        compiler_params=_cparams("parallel", "parallel"),
        name="attn_sample",
    )(q, k, v, kc, vc)


def _segsum(y2, bd_ref):
    hi = y2.astype(BF16)
    lo = (y2 - hi.astype(F32)).astype(BF16)
    return (jnp.dot(hi, bd_ref[...], preferred_element_type=F32)
            + jnp.dot(lo, bd_ref[...], preferred_element_type=F32))


def _gdn_prep_kernel(x_ref, xp_ref, xn_ref, cw_ref, bd_ref, o_ref, *, tiles_per_seq_sample, prompt_tiles):
    i = pl.program_id(0)
    x = x_ref[...]
    tb = x.shape[0]
    in_sample = i >= prompt_tiles
    pos = (i - prompt_tiles) % tiles_per_seq_sample
    first = jnp.logical_or(jnp.logical_not(in_sample), pos == 0)
    last = jnp.logical_or(jnp.logical_not(in_sample), pos == tiles_per_seq_sample - 1)
    prev_row = xp_ref[7:8, :] * jnp.where(first, 0.0, 1.0)
    next_row = xn_ref[0:1, :] * jnp.where(last, 0.0, 1.0)
    ri = lax.broadcasted_iota(jnp.int32, x.shape, 0)
    xprev = jnp.where(ri == 0, prev_row, pltpu.roll(x, 1, 0))
    xnext = jnp.where(ri == tb - 1, next_row, pltpu.roll(x, tb - 1, 0))
    y = _silu(xprev * cw_ref[0:1, :] + x * cw_ref[1:2, :] + xnext * cw_ref[2:3, :])
    q = y[:, 0:512]
    k = y[:, 512:1024]
    o_ref[:, 0:512] = q * lax.rsqrt(_segsum(q * q, bd_ref) + EPS) * (GDN_DK ** -0.5)
    o_ref[:, 512:1024] = k * lax.rsqrt(_segsum(k * k, bd_ref) + EPS)
    o_ref[:, 1024:1536] = y[:, 1024:1536]


def _gdn_prep_call(gqkv, conv_w8, bd):
    tb = SCAN_ROWS
    n = T_ALL // tb
    r8 = tb // 8
    nb8 = T_ALL // 8
    return pl.pallas_call(
        functools.partial(_gdn_prep_kernel, tiles_per_seq_sample=DEC_SEQ // tb, prompt_tiles=NP_ROWS // tb),
        grid=(n,),
        in_specs=[pl.BlockSpec((tb, 1536), lambda i: (i, 0)),
                  pl.BlockSpec((8, 1536), lambda i: (jnp.maximum(i * r8 - 1, 0), 0)),
                  pl.BlockSpec((8, 1536), lambda i: (jnp.minimum((i + 1) * r8, nb8 - 1), 0)),
                  _full((8, 1536)), _full((512, 512))],
        out_specs=pl.BlockSpec((tb, 1536), lambda i: (i, 0)),
        out_shape=jax.ShapeDtypeStruct((T_ALL, 1536), F32),
        compiler_params=_cparams("parallel"),
        name="gdn_prep",
    )(gqkv, gqkv, gqkv, conv_w8, bd)


def _pair_masks(reverse):
    ri = lax.broadcasted_iota(jnp.int32, (CH, LANES), 0)
    li = lax.broadcasted_iota(jnp.int32, (CH, LANES), 1)
    jm = jnp.bitwise_and(li, HALF - 1)
    tri = (ri <= jm) if reverse else (ri >= jm)
    strict = (ri < jm) if reverse else (ri > jm)
    diag = ri == jm
    left = li < HALF
    r2 = lax.broadcasted_iota(jnp.int32, (2 * CH, LANES), 0)
    l2 = lax.broadcasted_iota(jnp.int32, (2 * CH, LANES), 1)
    bdm = (r2 < HALF) == (l2 < HALF)
    return tri, strict, diag, left, bdm


def _bd(x, bdm):
    return jnp.where(bdm, jnp.concatenate([x, x], axis=0), 0.0)


def _row_form(col, diag):
    return jnp.sum(jnp.where(diag, col, 0.0), axis=0, keepdims=True)


def _gdn_scan_kernel(x_ref, g_ref, s0_ref, sel_ref, tri_ref, o_ref, sfin_ref, s_scr, *, reverse, n_chunks, inv_dot):
    c_idx = pl.program_id(1)

    @pl.when(c_idx == 0)
    def _():
        s_scr[...] = s0_ref[...]

    tri, strict, diag, _, bdm = _pair_masks(reverse)
    eye = jnp.where(diag, 1.0, 0.0)
    last = 0 if reverse else CH - 1

    def chunk(ci, carry):
        c = (n_chunks - 1 - ci) if reverse else ci
        rows = pl.ds(pl.multiple_of(c * CH, CH), CH)
        ex = _dot_hi(g_ref[rows, :], sel_ref[...])
        gcs = _dot_hi(tri_ref[...], ex[:, 0:512])
        for p in range(GDN_HEADS // 2):
            psl = slice(p * LANES, (p + 1) * LANES)
            q2 = x_ref[rows, p * LANES:(p + 1) * LANES]
            k2 = x_ref[rows, 512 + p * LANES:512 + (p + 1) * LANES]
            v2 = x_ref[rows, 1024 + p * LANES:1024 + (p + 1) * LANES]
            gcol = gcs[:, psl]
            bcol = ex[:, 512 + p * LANES:512 + (p + 1) * LANES]
            grow = _row_form(gcol, diag)
            dec = jnp.exp(jnp.where(tri, gcol - grow, -jnp.inf))
            kb = k2 * bcol
            ktbd = _bd(k2, bdm).T
            aq = _dot(jnp.concatenate([kb, q2], axis=0), ktbd)
            a = jnp.where(strict, aq[0:CH] * dec, 0.0)
            attn = aq[CH:2 * CH] * dec
            pw = -a
            t = eye + pw
            for _ in range(5):
                pw = inv_dot(pw, _bd(pw, bdm))
                t = t + inv_dot(t, _bd(pw, bdm))
            eg = jnp.exp(gcol)
            uw = _dot(t, jnp.concatenate([_bd(v2 * bcol, bdm), _bd(kb * eg, bdm)], axis=1))
            s = s_scr[p]
            wq = _dot(jnp.concatenate([uw[:, LANES:2 * LANES], q2 * eg], axis=0), s)
            vn = uw[:, 0:LANES] - wq[0:CH]
            vnbd = _bd(vn, bdm)
            o_ref[rows, psl] = wq[CH:2 * CH] + _dot(attn, vnbd)
            glrow = gcol[last:last + 1, :]
            s_scr[p] = s * jnp.exp(glrow) + _dot(ktbd * jnp.exp(glrow - grow), vnbd)
        return carry

    lax.fori_loop(0, n_chunks, chunk, 0)

    @pl.when(c_idx == pl.num_programs(1) - 1)
    def _():
        sfin_ref[...] = s_scr[...]


def _gdn_scan_call(xq, gates, s0, sel, tri, *, n_seq, seq_len, row_off, reverse, inv_dot):
    nblk = seq_len // SCAN_ROWS
    boff = row_off // SCAN_ROWS

    def rmap(b, j):
        jj = (nblk - 1 - j) if reverse else j
        return (boff + b * nblk + jj, 0)

    def omap(b, j):
        jj = (nblk - 1 - j) if reverse else j
        return (b * nblk + jj, 0)

    return pl.pallas_call(
        functools.partial(_gdn_scan_kernel, reverse=reverse, n_chunks=SCAN_ROWS // CH, inv_dot=inv_dot),
        grid=(n_seq, nblk),
        in_specs=[pl.BlockSpec((SCAN_ROWS, 1536), rmap),
                  pl.BlockSpec((SCAN_ROWS, LANES), rmap),
                  pl.BlockSpec((None, 4, LANES, LANES), lambda b, j: (b, 0, 0, 0)),
                  _full(sel.shape), _full((CH, CH))],
        out_specs=[pl.BlockSpec((SCAN_ROWS, 512), omap),
                   pl.BlockSpec((None, 4, LANES, LANES), lambda b, j: (b, 0, 0, 0))],
        out_shape=[jax.ShapeDtypeStruct((n_seq * seq_len, 512), F32),
                   jax.ShapeDtypeStruct((n_seq, 4, LANES, LANES), F32)],
        scratch_shapes=[pltpu.VMEM((4, LANES, LANES), F32)],
        compiler_params=_cparams("parallel", "arbitrary"),
        name="gdn_scan_bwd" if reverse else "gdn_scan_fwd",
    )(xq, gates, s0, sel, tri)


def _mlstm_scan_kernel(q_ref, k_ref, v_ref, g_ref, c0_ref, m0_ref, sel_ref, tri_ref,
                       o_ref, cfin_ref, mfin_ref, c_scr, m_scr, *, reverse, n_chunks):
    c_idx = pl.program_id(1)

    @pl.when(c_idx == 0)
    def _():
        c_scr[...] = c0_ref[...]
        m_scr[...] = m0_ref[...]

    tri, _, diag, left, bdm = _pair_masks(reverse)
    left1 = left[0:1, :]
    last = 0 if reverse else CH - 1
    ninf = -jnp.inf
    ones = jnp.ones((CH, LANES), F32)
    rtop = lax.broadcasted_iota(jnp.int32, (2 * CH, 2 * LANES), 0) < HALF

    def chunk(ci, carry):
        c = (n_chunks - 1 - ci) if reverse else ci
        rows = pl.ds(pl.multiple_of(c * CH, CH), CH)
        ex = _dot_hi(g_ref[rows, :], sel_ref[...])
        bfull = _dot_hi(tri_ref[...], ex[:, 1024:2048])
        for p in range(MLSTM_HEADS // 2):
            h0, h1 = 2 * p, 2 * p + 1
            psl = slice(p * LANES, (p + 1) * LANES)
            b0 = bfull[:, h0 * LANES:(h0 + 1) * LANES]
            b1 = bfull[:, h1 * LANES:(h1 + 1) * LANES]
            bcol = jnp.where(left, b0, b1)
            icol = jnp.where(left, ex[:, h0 * LANES:(h0 + 1) * LANES], ex[:, h1 * LANES:(h1 + 1) * LANES])
            brow = _row_form(bcol, diag)
            irow = _row_form(icol, diag)
            dlog = jnp.where(tri, bcol - brow + irow, ninf)
            dmax0 = jnp.max(jnp.where(left, dlog, ninf), axis=-1, keepdims=True)
            dmax1 = jnp.max(jnp.where(left, ninf, dlog), axis=-1, keepdims=True)
            m0 = m_scr[h0:h0 + 1, :]
            m1 = m_scr[h1:h1 + 1, :]
            inter0 = b0 + m0
            inter1 = b1 + m1
            mt0 = jnp.maximum(inter0, dmax0)
            mt1 = jnp.maximum(inter1, dmax1)
            q2 = q_ref[rows, psl]
            k2 = k_ref[rows, psl]
            ktbd = _bd(k2, bdm).T
            qk = _dot(q2, ktbd)
            wts = jnp.exp(dlog - jnp.where(left, mt0, mt1)) * qk
            qa = q2 * jnp.where(left, jnp.exp(inter0 - mt0), jnp.exp(inter1 - mt1))
            vx = jnp.concatenate(
                [jnp.concatenate([v_ref[rows, h0 * LANES:(h0 + 1) * LANES], ones], axis=1),
                 jnp.concatenate([v_ref[rows, h1 * LANES:(h1 + 1) * LANES], ones], axis=1)], axis=0)
            vxb = vx.astype(BF16)
            cx = c_scr[p]
            rhs = jnp.concatenate([vxb, cx.astype(BF16)], axis=0)
            for hh, (mt, hd) in enumerate(((mt0, h0), (mt1, h1))):
                keep = left if hh == 0 else jnp.logical_not(left)
                lhs = jnp.concatenate([jnp.where(keep, wts, 0.0), jnp.where(keep, qa, 0.0)], axis=1)
                xo = jnp.dot(lhs.astype(BF16), rhs, preferred_element_type=F32)
                den = jnp.maximum(jnp.abs(xo[:, LANES:2 * LANES]), jnp.exp(-mt))
                o_ref[rows, hd * LANES:(hd + 1) * LANES] = xo[:, 0:LANES] / den
            bl0 = b0[last:last + 1, :]
            bl1 = b1[last:last + 1, :]
            glog = jnp.where(left1, bl0, bl1) - brow + irow
            gmax0 = jnp.max(jnp.where(left1, glog, ninf), axis=-1, keepdims=True)
            gmax1 = jnp.max(jnp.where(left1, ninf, glog), axis=-1, keepdims=True)
            mn0 = jnp.maximum(bl0 + m0, gmax0)
            mn1 = jnp.maximum(bl1 + m1, gmax1)
            ws = jnp.exp(glog - jnp.where(left1, mn0, mn1))
            d0 = jnp.exp(bl0 + m0 - mn0)
            d1 = jnp.exp(bl1 + m1 - mn1)
            dcol = jnp.where(rtop, jnp.concatenate([d0, d0], axis=1), jnp.concatenate([d1, d1], axis=1))
            c_scr[p] = dcol * cx + jnp.dot((ktbd * ws).astype(BF16), vxb, preferred_element_type=F32)
            m_scr[h0:h0 + 1, :] = mn0
            m_scr[h1:h1 + 1, :] = mn1
        return carry

    lax.fori_loop(0, n_chunks, chunk, 0)

    @pl.when(c_idx == pl.num_programs(1) - 1)
    def _():
        cfin_ref[...] = c_scr[...]
        mfin_ref[...] = m_scr[...]


def _mlstm_scan_call(q, k, v, gates, c0, m0, sel, tri, *, n_seq, seq_len, row_off, reverse):
    nblk = seq_len // SCAN_ROWS
    boff = row_off // SCAN_ROWS

    def rmap(b, j):
        jj = (nblk - 1 - j) if reverse else j
        return (boff + b * nblk + jj, 0)

    def omap(b, j):
        jj = (nblk - 1 - j) if reverse else j
        return (b * nblk + jj, 0)

    st = lambda *shape: pl.BlockSpec((None,) + shape, lambda b, j: (b,) + (0,) * len(shape))
    return pl.pallas_call(
        functools.partial(_mlstm_scan_kernel, reverse=reverse, n_chunks=SCAN_ROWS // CH),
        grid=(n_seq, nblk),
        in_specs=[pl.BlockSpec((SCAN_ROWS, 512), rmap), pl.BlockSpec((SCAN_ROWS, 512), rmap),
                  pl.BlockSpec((SCAN_ROWS, 1024), rmap), pl.BlockSpec((SCAN_ROWS, LANES), rmap),
                  st(4, LANES, 2 * LANES), st(8, LANES), _full(sel.shape), _full((CH, CH))],
        out_specs=[pl.BlockSpec((SCAN_ROWS, 1024), omap), st(4, LANES, 2 * LANES), st(8, LANES)],
        out_shape=[jax.ShapeDtypeStruct((n_seq * seq_len, 1024), F32),
                   jax.ShapeDtypeStruct((n_seq, 4, LANES, 2 * LANES), F32),
                   jax.ShapeDtypeStruct((n_seq, 8, LANES), F32)],
        scratch_shapes=[pltpu.VMEM((4, LANES, 2 * LANES), F32), pltpu.VMEM((8, LANES), F32)],
        compiler_params=_cparams("parallel", "arbitrary"),
        name="mlstm_scan_bwd" if reverse else "mlstm_scan_fwd",
    )(q, k, v, gates, c0, m0, sel, tri)


def _odd_in_kernel(x_ref, m_ref, nw_ref, w_ref, gb_ref, q_ref, k_ref, v_ref, og_ref, gate_ref):
    h = _norm_mod(x_ref[...], nw_ref[...], m_ref[1:2, :], m_ref[0:1, :])
    proj = jnp.dot(h.astype(BF16), w_ref[...], preferred_element_type=F32)
    q_ref[...] = proj[:, 0:512]
    k_ref[...] = proj[:, 512:1024] * (MLSTM_DK ** -0.5)
    v_ref[...] = proj[:, 1024:2048]
    og_ref[...] = proj[:, 2048:3072]
    g = proj[:, 3072:3200] + gb_ref[...]
    lane = lax.broadcasted_iota(jnp.int32, g.shape, 1)
    gate_ref[...] = jnp.where(lane < 16, g, -_softplus(-g))


def _odd_in_call(x, mod3, nw, w, gbias):
    tm = TM_IN
    nt = GROUP // tm
    row = lambda w_: pl.BlockSpec((tm, w_), lambda i: (i, 0))
    outs = [512, 512, 1024, 1024, 128]
    return pl.pallas_call(
        _odd_in_kernel,
        grid=(T_ALL // tm,),
        in_specs=[row(D_MODEL), pl.BlockSpec((None, 8, D_MODEL), lambda i: (i // nt, 0, 0)),
                  _full((1, D_MODEL)), _full(w.shape), _full((1, LANES))],
        out_specs=[row(w_) for w_ in outs],
        out_shape=[jax.ShapeDtypeStruct((T_ALL, w_), F32) for w_ in outs],
        compiler_params=_cparams("parallel"),
        name="odd_in",
    )(x, mod3, nw, w, gbias)


def _ffn_kernel(*refs, even):
    if even:
        (x_ref, m_ref, nw_ref, attn_ref, of_ref, ob_ref, gz_ref, onw_ref, bd_ref, wo_ref, win_ref, wout_ref,
         o_ref, x1_scr, hn_scr, acc_scr) = refs
    else:
        (x_ref, m_ref, nw_ref, hf_ref, hb_ref, og_ref, onw_ref, wo_ref, win_ref, wout_ref,
         o_ref, x1_scr, hn_scr, acc_scr) = refs
    j = pl.program_id(1)

    @pl.when(j == 0)
    def _():
        if even:
            d = of_ref[...] + ob_ref[...]
            ms = _segsum(d * d, bd_ref) * (1.0 / GDN_DV)
            dn = d * lax.rsqrt(ms + EPS) * onw_ref[...] * _silu(gz_ref[...])
            mixed = (jnp.dot(attn_ref[...], wo_ref[0:512, :], preferred_element_type=F32)
                     + jnp.dot(dn.astype(BF16), wo_ref[512:1024, :], preferred_element_type=F32))
        else:
            d = hf_ref[...] + hb_ref[...]
            parts = []
            for h in range(MLSTM_HEADS):
                dh = d[:, h * LANES:(h + 1) * LANES]
                parts.append(dh * lax.rsqrt(jnp.mean(dh * dh, axis=-1, keepdims=True) + EPS))
            mem = jnp.concatenate(parts, axis=1) * onw_ref[...] * jax.nn.sigmoid(og_ref[...])
            mixed = jnp.dot(mem.astype(BF16), wo_ref[...], preferred_element_type=F32)
        x1 = x_ref[...] + m_ref[2:3, :] * mixed
        x1_scr[...] = x1
        hn_scr[...] = _norm_mod(x1, nw_ref[...], m_ref[4:5, :], m_ref[3:4, :]).astype(BF16)
        acc_scr[...] = jnp.zeros_like(acc_scr)

    gu = jnp.dot(hn_scr[...], win_ref[...], preferred_element_type=F32)
    act = _silu(gu[:, 0:TF]) * gu[:, TF:2 * TF]
    acc_scr[...] += jnp.dot(act.astype(BF16), wout_ref[...], preferred_element_type=F32)

    @pl.when(j == pl.num_programs(1) - 1)
    def _():
        o_ref[...] = x1_scr[...] + m_ref[5:6, :] * acc_scr[...]


def _ffn_call(even, x, mod3, nw, mix_inputs, onw, bd, wo, win, wout):
    tm = TM_FFN
    nt = GROUP // tm
    nf = D_FF // TF
    row = lambda w_: pl.BlockSpec((tm, w_), lambda i, j: (i, 0))
    const = lambda shape: pl.BlockSpec(shape, lambda i, j: (0,) * len(shape))
    in_specs = [row(D_MODEL), pl.BlockSpec((None, 8, D_MODEL), lambda i, j: (i // nt, 0, 0)), const((1, D_MODEL))]
    in_specs += [row(a.shape[1]) for a in mix_inputs]
    args = [x, mod3, nw] + list(mix_inputs) + [onw]
    in_specs.append(const(onw.shape))
    if even:
        in_specs.append(const(bd.shape))
        args.append(bd)
    in_specs += [const(wo.shape),
                 pl.BlockSpec((None, D_MODEL, 2 * TF), lambda i, j: (j, 0, 0)),
                 pl.BlockSpec((TF, D_MODEL), lambda i, j: (j, 0))]
    args += [wo, win, wout]
    return pl.pallas_call(
        functools.partial(_ffn_kernel, even=even),
        grid=(T_ALL // tm, nf),
        in_specs=in_specs,
        out_specs=row(D_MODEL),
        out_shape=jax.ShapeDtypeStruct((T_ALL, D_MODEL), F32),
        scratch_shapes=[pltpu.VMEM((tm, D_MODEL), F32), pltpu.VMEM((tm, D_MODEL), BF16),
                        pltpu.VMEM((tm, D_MODEL), F32)],
        compiler_params=_cparams("parallel", "arbitrary"),
        name="ffn_even" if even else "ffn_odd",
    )(*args)


def _rope_tables():
    rows = DEC_SEQ // GRID_W
    row = jnp.repeat(jnp.arange(rows, dtype=F32), GRID_W)
    col = jnp.tile(jnp.arange(GRID_W, dtype=F32), rows)
    per_axis = MLA_ROPE // 2
    inv = 1.0 / (ROPE_BASE ** (jnp.arange(0, per_axis, 2, dtype=F32) / per_axis))
    ar = row[:, None] * inv
    ac = col[:, None] * inv
    ang = jnp.concatenate([ar, ar, ac, ac], axis=-1)
    cos, sin = jnp.cos(ang), jnp.sin(ang)
    first = np.tile(np.concatenate([np.ones(8), np.zeros(8)]), 2).astype(np.float32)
    pad = lambda a, fill: jnp.concatenate(
        [jnp.full((DEC_SEQ, MLA_NOPE), fill, F32), a, jnp.full((DEC_SEQ, LANES - MLA_QK), fill, F32)], axis=1)
    cos_t = pad(cos, 1.0)
    sa_t = pad(-sin * first, 0.0)
    sb_t = pad(sin * (1.0 - first), 0.0)
    ident = lambda fill: jnp.full((GROUP, LANES), fill, F32)
    return (jnp.concatenate([ident(1.0), cos_t]), jnp.concatenate([ident(0.0), sa_t]),
            jnp.concatenate([ident(0.0), sb_t]))


def _pad_cols(a, left, total):
    return jnp.pad(a, ((0, 0), (left, total - left - a.shape[1])))


def _lane_row(vec, total=LANES):
    return jnp.pad(vec.reshape(1, -1).astype(F32), ((0, 0), (0, total - vec.size)))


def _gdn_sel(direction):
    j = np.arange(512)
    head = 2 * (j // LANES) + (j % LANES >= HALF)
    sel = np.zeros((LANES, 1024), np.float32)
    sel[direction * 8 + head, j] = 1.0
    sel[16 + direction * 8 + head, 512 + j] = 1.0
    return jnp.asarray(sel)


def _mlstm_sel(direction):
    j = np.arange(1024)
    sel = np.zeros((LANES, 2048), np.float32)
    sel[direction * 8 + j // LANES, j] = 1.0
    sel[16 + direction * 8 + j // LANES, 1024 + j] = 1.0
    return jnp.asarray(sel)


def _tri(reverse):
    t = np.tril(np.ones((CH, CH), np.float32))
    return jnp.asarray(t.T if reverse else t)


def _block_diag_pairs(s):
    lead = s.shape[:-3]
    s = s.reshape(lead + (4, 2, HALF, HALF))
    z = jnp.zeros_like(s[..., 0, :, :])
    top = jnp.concatenate([s[..., 0, :, :], z], axis=-1)
    bot = jnp.concatenate([z, s[..., 1, :, :]], axis=-1)
    return jnp.concatenate([top, bot], axis=-2)


def _unblock_diag_pairs(s):
    a = s[..., :HALF, :HALF]
    b = s[..., HALF:, HALF:]
    return jnp.stack([a, b], axis=-3).reshape(s.shape[:-3] + (8, HALF, HALF))


def _inv_dot_bf16(a, b):
    return _dot(a, b)


def _inv_dot_f32(a, b):
    return _dot_hi(a, b)


INV_DOT = _inv_dot_f32


def kernel(x_prompt, x_sample, c, c_ctx, cache_mla_ckv, cache_mla_krope, state_gdn, state_mlstm_C, state_mlstm_n,
           state_mlstm_m, norm_mix, norm_ffn, w_ada, b_ada, w_ffn_in, w_ffn_out, w_even_in, w_even_out,
           mla_q_a_norm, mla_kv_a_norm, w_mla_uq, w_mla_ukv, mla_q_norm, mla_k_norm, gdn_conv, gdn_a_log,
           gdn_dt_bias, gdn_out_norm, w_odd_in, w_odd_out, mlstm_gate_bias, mlstm_out_norm):
    x = jnp.concatenate([x_prompt.reshape(NP_ROWS, D_MODEL), x_sample.reshape(NS_ROWS, D_MODEL)], axis=0)

    cond8 = jnp.concatenate([c_ctx[None, :], c, jnp.zeros((8 - 1 - DEC_BATCH, D_MODEL), F32)], axis=0)
    mods = _ada_call(cond8, w_ada, b_ada)
    mods = mods[:, :3].reshape(DEPTH, 3, 6, D_MODEL)
    mods = jnp.pad(mods, ((0, 0), (0, 0), (0, 2), (0, 0)))

    cos_t, sa_t, sb_t = _rope_tables()
    bd512 = jnp.asarray(np.kron(np.eye(8, dtype=np.float32), np.ones((HALF, HALF), np.float32))).astype(BF16)
    tri_f, tri_b = _tri(False), _tri(True)

    ckv_new, krope_new, gdn_new, mc_new, mn_new, mm_new = [], [], [], [], [], []
    for l in range(DEPTH):
        j = l // 2
        mod3 = mods[l]
        win_l = w_ffn_in[l].astype(BF16)
        win_t = jnp.concatenate([win_l[:, :D_FF].reshape(D_MODEL, D_FF // TF, TF),
                                 win_l[:, D_FF:].reshape(D_MODEL, D_FF // TF, TF)], axis=-1).transpose(1, 0, 2)
        wout_l = w_ffn_out[l].astype(BF16)
        nw1 = norm_mix[l].reshape(1, D_MODEL)
        nw2 = norm_ffn[l].reshape(1, D_MODEL)
        if l % 2 == 0:
            wi = w_even_in[j].astype(BF16)
            w_arr = jnp.concatenate([wi[:, 0:640], _pad_cols(wi[:, 640:672], MLA_NOPE, LANES), wi[:, 672:2720],
                                     _pad_cols(wi[:, 2720:2752], 0, LANES)], axis=1)
            wuq = jnp.pad(w_mla_uq[j].astype(BF16).reshape(Q_LORA, MLA_HEADS, MLA_QK),
                          ((0, 0), (0, 0), (0, LANES - MLA_QK))).reshape(Q_LORA, MLA_HEADS * LANES)
            wukv = w_mla_ukv[j].astype(BF16).reshape(KV_LORA, MLA_HEADS, MLA_NOPE + MLA_V)
            wk = jnp.pad(wukv[:, :, :MLA_NOPE], ((0, 0), (0, 0), (0, LANES - MLA_NOPE))).reshape(KV_LORA, -1)
            wv = wukv[:, :, MLA_NOPE:].reshape(KV_LORA, MLA_HEADS * MLA_V)
            qn = _lane_row(mla_q_norm[j])
            kn = _lane_row(mla_k_norm[j])
            gp = jnp.concatenate([_lane_row(gdn_a_log[j]), _lane_row(gdn_dt_bias[j]), jnp.zeros((6, LANES), F32)])
            q, k, v, ckvn, kr, gqkv, gz, gates = _even_in_call(
                x, mod3, nw1, w_arr, mla_q_a_norm[j].reshape(1, -1), mla_kv_a_norm[j].reshape(1, -1),
                wuq, wk, wv, qn, kn, cos_t, sa_t, sb_t, gp)
            ckv_new.append(ckvn[:NP_ROWS].reshape(BATCH, SEQ, KV_LORA))
            krope_new.append(kr[:NP_ROWS, MLA_NOPE:MLA_QK].reshape(BATCH, SEQ, MLA_ROPE))
            kc, vc = _ctx_kv_call(cache_mla_ckv[:, j].reshape(DEC_BATCH * PAST_LEN, KV_LORA),
                                  _pad_cols(cache_mla_krope[:, j].reshape(DEC_BATCH * PAST_LEN, MLA_ROPE),
                                            MLA_NOPE, LANES), wk, wv, kn)
            attn = jnp.concatenate([_attn_prompt_call(q, k, v), _attn_sample_call(q, k, v, kc, vc)], axis=0)
            conv8 = jnp.pad(gdn_conv[j], ((0, 5), (0, 0)))
            xq = _gdn_prep_call(gqkv, conv8, bd512)
            zeros_p = jnp.zeros((BATCH, 4, LANES, LANES), F32)
            o_dirs, s_dirs = [], []
            for d, (rev, tri_d) in enumerate(((False, tri_f), (True, tri_b))):
                sel = _gdn_sel(d)
                s0 = _block_diag_pairs(state_gdn[:, j, d])
                op, sp = _gdn_scan_call(xq, gates, zeros_p, sel, tri_d, n_seq=BATCH, seq_len=SEQ, row_off=0,
                                        reverse=rev, inv_dot=INV_DOT)
                os_, _ = _gdn_scan_call(xq, gates, s0, sel, tri_d, n_seq=DEC_BATCH, seq_len=DEC_SEQ,
                                        row_off=NP_ROWS, reverse=rev, inv_dot=INV_DOT)
                o_dirs.append(jnp.concatenate([op, os_], axis=0))
                s_dirs.append(_unblock_diag_pairs(sp))
            gdn_new.append(jnp.stack(s_dirs, axis=1))
            onw = jnp.tile(gdn_out_norm[j], GDN_HEADS).reshape(1, -1)
            x = _ffn_call(True, x, mod3, nw2, [attn, o_dirs[0], o_dirs[1], gz], onw, bd512,
                          w_even_out[j].astype(BF16), win_t, wout_l)
        else:
            wi = w_odd_in[j].astype(BF16)
            w_arr = jnp.concatenate([wi[:, :3072], _pad_cols(wi[:, 3072:3104], 0, LANES)], axis=1)
            q, k, v, og, gates = _odd_in_call(x, mod3, nw1, w_arr, _lane_row(mlstm_gate_bias[j]))
            h_dirs, c_dirs, m_dirs = [], [], []
            for d, (rev, tri_d) in enumerate(((False, tri_f), (True, tri_b))):
                sel = _mlstm_sel(d)
                c0 = state_mlstm_C[:, j, d]
                n0 = state_mlstm_n[:, j, d]
                cx0 = jnp.concatenate([c0, jnp.broadcast_to(n0[..., None], c0.shape)], axis=-1)
                cx0 = cx0.reshape(DEC_BATCH, 4, LANES, 2 * LANES)
                m0 = jnp.broadcast_to(state_mlstm_m[:, j, d][..., None], (DEC_BATCH, 8, LANES))
                hp, cp, mp = _mlstm_scan_call(q, k, v, gates, jnp.zeros((BATCH, 4, LANES, 2 * LANES), F32),
                                              jnp.zeros((BATCH, 8, LANES), F32), sel, tri_d,
                                              n_seq=BATCH, seq_len=SEQ, row_off=0, reverse=rev)
                hs, _, _ = _mlstm_scan_call(q, k, v, gates, cx0, m0, sel, tri_d,
                                            n_seq=DEC_BATCH, seq_len=DEC_SEQ, row_off=NP_ROWS, reverse=rev)
                h_dirs.append(jnp.concatenate([hp, hs], axis=0))
                c_dirs.append(cp.reshape(BATCH, 8, HALF, 2 * LANES))
                m_dirs.append(mp[:, :, 0])
            cst = jnp.stack(c_dirs, axis=1)
            mc_new.append(cst[..., :LANES])
            mn_new.append(cst[..., LANES])
            mm_new.append(jnp.stack(m_dirs, axis=1))
            x = _ffn_call(False, x, mod3, nw2, [h_dirs[0], h_dirs[1], og], mlstm_out_norm[j].reshape(1, -1), None,
                          w_odd_out[j].astype(BF16), win_t, wout_l)

    return (x[:NP_ROWS].reshape(BATCH, SEQ, D_MODEL), x[NP_ROWS:].reshape(DEC_BATCH, DEC_SEQ, D_MODEL),
            jnp.stack(ckv_new, axis=1), jnp.stack(krope_new, axis=1), jnp.stack(gdn_new, axis=1),
            jnp.stack(mc_new, axis=1), jnp.stack(mn_new, axis=1), jnp.stack(mm_new, axis=1))
```

```python
import functools

import jax
import jax.numpy as jnp
import numpy as np
from jax import lax
from jax.experimental import pallas as pl
from jax.experimental.pallas import tpu as pltpu

F32 = jnp.float32
BF16 = jnp.bfloat16
HIGHEST = lax.Precision.HIGHEST

D_MODEL = 1024
BATCH = 16
SEQ = 256
DEPTH = 4
DEC_BATCH = 2
DEC_SEQ = 4096
PAST_LEN = 256
GRID_W = 64
EPS = 1e-6
ROPE_BASE = 10000.0
MLA_HEADS = 8
MLA_NOPE = 64
MLA_ROPE = 32
MLA_QK = MLA_NOPE + MLA_ROPE
MLA_V = 64
Q_LORA = 384
KV_LORA = 256
GDN_HEADS = 8
GDN_DK = 64
GDN_DV = 64
MLSTM_HEADS = 8
MLSTM_DK = 64
MLSTM_DV = 128
D_FF = 2816

LANES = 128
HALF = 64
CH = 64
NP_ROWS = BATCH * SEQ
NS_ROWS = DEC_BATCH * DEC_SEQ
T_ALL = NP_ROWS + NS_ROWS
GROUP = 4096
assert NP_ROWS == GROUP and DEC_SEQ == GROUP

TM_IN = 256
TM_FFN = 512
TF = 256
TQ = 256
SCAN_ROWS = 256
VMEM_LIMIT = 56 * 1024 * 1024

EVEN_W = 384 + 256 + 128 + 1536 + 512 + 128
ODD_W = 512 + 512 + 1024 + 1024 + 128


def _cparams(*sem):
    return pltpu.CompilerParams(dimension_semantics=sem, vmem_limit_bytes=VMEM_LIMIT)


def _silu(x):
    return x * jax.nn.sigmoid(x)


def _softplus(x):
    return jnp.maximum(x, 0.0) + jnp.log(1.0 + jnp.exp(-jnp.abs(x)))


def _dot(a, b):
    return jnp.dot(a.astype(BF16), b.astype(BF16), preferred_element_type=F32)


def _dot_hi(a, b):
    return jnp.dot(a, b, precision=HIGHEST, preferred_element_type=F32)


def _split3(x):
    hi = x.astype(BF16)
    r = x - hi.astype(F32)
    mid = r.astype(BF16)
    lo = (r - mid.astype(F32)).astype(BF16)
    return hi, mid, lo


def _split2(x):
    hi = x.astype(BF16)
    return hi, (x - hi.astype(F32)).astype(BF16)


def _dot_x3(a, b):
    f = lambda x, y: jnp.dot(x, y, preferred_element_type=F32)
    return f(a[0], b[0]) + f(a[0], b[1]) + f(a[1], b[0])


def _dot_x01(x, m01):
    return sum(jnp.dot(part, m01, preferred_element_type=F32) for part in _split3(x))


def _dot_01x(m01, x):
    return sum(jnp.dot(m01, part, preferred_element_type=F32) for part in _split3(x))


def _norm_mod(x, nw, sc, sh):
    ms = jnp.mean(x * x, axis=-1, keepdims=True)
    return (x * lax.rsqrt(ms + EPS) * nw) * (1.0 + sc) + sh


def _ada_kernel(c_ref, w_ref, b_ref, o_ref):
    c = c_ref[...]
    o_ref[...] = _dot(_silu(c), w_ref[...]) + b_ref[...]


def _ada_call(cond8, w_ada, b_ada):
    tn = 1536
    n = w_ada.shape[-1]
    return pl.pallas_call(
        _ada_kernel,
        grid=(DEPTH, n // tn),
        in_specs=[pl.BlockSpec((8, D_MODEL), lambda l, j: (0, 0)),
                  pl.BlockSpec((None, D_MODEL, tn), lambda l, j: (l, 0, j)),
                  pl.BlockSpec((None, 1, tn), lambda l, j: (l, 0, j))],
        out_specs=pl.BlockSpec((None, 8, tn), lambda l, j: (l, 0, j)),
        out_shape=jax.ShapeDtypeStruct((DEPTH, 8, n), F32),
        compiler_params=_cparams("parallel", "parallel"),
        name="ada",
    )(cond8, w_ada, b_ada.reshape(DEPTH, 1, n))


def _rope(x, cos, sa, sb):
    return x * cos + pltpu.roll(x, LANES - 8, 1) * sa + pltpu.roll(x, 8, 1) * sb


def _head_norm_rope(x, w, rope, scale):
    ms = jnp.sum(x * x, axis=-1, keepdims=True) * (1.0 / MLA_QK)
    y = x * lax.rsqrt(ms + EPS) * w
    if rope is not None:
        y = _rope(y, *rope)
    if scale != 1.0:
        y = y * scale
    return y


def _mla_kv(ckvn, kr, wk_ref, wv_ref, kn, rope, k_ref, v_ref):
    cb = ckvn.astype(BF16)
    kraw = jnp.dot(cb, wk_ref[...], preferred_element_type=F32)
    for h in range(MLA_HEADS):
        sl = slice(h * LANES, (h + 1) * LANES)
        k_ref[:, sl] = _head_norm_rope(kraw[:, sl] + kr, kn, rope, 1.0).astype(BF16)
    v_ref[...] = jnp.dot(cb, wv_ref[...], preferred_element_type=F32).astype(BF16)


def _even_in_kernel(x_ref, m_ref, nw_ref, w_ref, qan_ref, kvan_ref, wuq_ref, wk_ref, wv_ref, qn_ref, kn_ref,
                    cos_ref, sa_ref, sb_ref, gp_ref,
                    q_ref, k_ref, v_ref, ckv_ref, kr_ref, gqkv_ref, gz_ref, gate_ref):
    x = x_ref[...]
    h = _norm_mod(x, nw_ref[...], m_ref[1:2, :], m_ref[0:1, :])
    proj = jnp.dot(h.astype(BF16), w_ref[...], preferred_element_type=F32)
    rope = (cos_ref[...], sa_ref[...], sb_ref[...])
    cq = proj[:, 0:384]
    cqn = cq * lax.rsqrt(jnp.mean(cq * cq, axis=-1, keepdims=True) + EPS) * qan_ref[...]
    qraw = jnp.dot(cqn.astype(BF16), wuq_ref[...], preferred_element_type=F32)
    qn = qn_ref[...]
    for hd in range(MLA_HEADS):
        sl = slice(hd * LANES, (hd + 1) * LANES)
        q_ref[:, sl] = _head_norm_rope(qraw[:, sl], qn, rope, MLA_QK ** -0.5).astype(BF16)
    ckv = proj[:, 384:640]
    ckvn = ckv * lax.rsqrt(jnp.mean(ckv * ckv, axis=-1, keepdims=True) + EPS) * kvan_ref[...]
    ckv_ref[...] = ckvn
    kr = proj[:, 640:768]
    kr_ref[...] = kr
    _mla_kv(ckvn, kr, wk_ref, wv_ref, kn_ref[...], rope, k_ref, v_ref)
    gqkv_ref[...] = proj[:, 768:2304]
    gz_ref[...] = proj[:, 2304:2816]
    g = proj[:, 2816:2944]
    lane = lax.broadcasted_iota(jnp.int32, g.shape, 1)
    log_decay = -jnp.exp(gp_ref[0:1, :]) * _softplus(g + gp_ref[1:2, :])
    gate_ref[...] = jnp.where(lane < 16, log_decay, jax.nn.sigmoid(g))


def _full(shape):
    nd = len(shape)
    return pl.BlockSpec(shape, lambda *_: (0,) * nd)


def _even_in_call(x, mod3, nw, w, qan, kvan, wuq, wk, wv, qn, kn, cos, sa, sb, gp):
    tm = TM_IN
    nt = GROUP // tm
    row = lambda w_: pl.BlockSpec((tm, w_), lambda i: (i, 0))
    rope_spec = pl.BlockSpec((tm, LANES), lambda i: (jnp.where(i < nt, i, nt + i % nt), 0))
    outs = [(1024, BF16), (1024, BF16), (512, BF16), (256, F32), (128, F32), (1536, F32), (512, F32), (128, F32)]
    return pl.pallas_call(
        _even_in_kernel,
        grid=(T_ALL // tm,),
        in_specs=[row(D_MODEL),
                  pl.BlockSpec((None, 8, D_MODEL), lambda i: (i // nt, 0, 0)),
                  _full((1, D_MODEL)), _full(w.shape), _full((1, 384)), _full((1, 256)),
                  _full(wuq.shape), _full(wk.shape), _full(wv.shape), _full((1, LANES)), _full((1, LANES)),
                  rope_spec, rope_spec, rope_spec, _full((8, LANES))],
        out_specs=[row(w_) for w_, _ in outs],
        out_shape=[jax.ShapeDtypeStruct((T_ALL, w_), dt) for w_, dt in outs],
        compiler_params=_cparams("parallel"),
        name="even_in",
    )(x, mod3, nw, w, qan, kvan, wuq, wk, wv, qn, kn, cos, sa, sb, gp)


def _ctx_kv_kernel(ckv_ref, kr_ref, wk_ref, wv_ref, kn_ref, k_ref, v_ref):
    _mla_kv(ckv_ref[...], kr_ref[...], wk_ref, wv_ref, kn_ref[...], None, k_ref, v_ref)


def _ctx_kv_call(ckv, kr128, wk, wv, kn):
    n = ckv.shape[0]
    return pl.pallas_call(
        _ctx_kv_kernel,
        grid=(1,),
        in_specs=[_full(ckv.shape), _full(kr128.shape), _full(wk.shape), _full(wv.shape), _full((1, LANES))],
        out_specs=[_full((n, 1024)), _full((n, 512))],
        out_shape=[jax.ShapeDtypeStruct((n, 1024), BF16), jax.ShapeDtypeStruct((n, 512), BF16)],
        compiler_params=_cparams("arbitrary"),
        name="ctx_kv",
    )(ckv, kr128, wk, wv, kn)


def _attn_kernel(*refs, has_ctx):
    if has_ctx:
        q_ref, k_ref, v_ref, kc_ref, vc_ref, o_ref = refs
    else:
        q_ref, k_ref, v_ref, o_ref = refs
    nt = (((1,), (1,)), ((), ()))
    for p in range(MLA_HEADS // 2):
        psl = slice(p * LANES, (p + 1) * LANES)
        v2 = v_ref[:, psl]
        lane = lax.broadcasted_iota(jnp.int32, v2.shape, 1)
        vs = (jnp.where(lane < HALF, v2, jnp.zeros_like(v2)), jnp.where(lane >= HALF, v2, jnp.zeros_like(v2)))
        if has_ctx:
            vc2 = vc_ref[:, psl]
            lc = lax.broadcasted_iota(jnp.int32, vc2.shape, 1)
            vcs = (jnp.where(lc < HALF, vc2, jnp.zeros_like(vc2)), jnp.where(lc >= HALF, vc2, jnp.zeros_like(vc2)))
        acc = None
        for hh in range(2):
            sl = slice((2 * p + hh) * LANES, (2 * p + hh + 1) * LANES)
            qh = q_ref[:, sl]
            s = lax.dot_general(qh, k_ref[:, sl], nt, preferred_element_type=F32)
            m = jnp.max(s, axis=-1, keepdims=True)
            if has_ctx:
                sc = lax.dot_general(qh, kc_ref[:, sl], nt, preferred_element_type=F32)
                m = jnp.maximum(m, jnp.max(sc, axis=-1, keepdims=True))
            e = jnp.exp(s - m)
            l = jnp.sum(e, axis=-1, keepdims=True)
            o = jnp.dot(e.astype(BF16), vs[hh], preferred_element_type=F32)
            if has_ctx:
                ec = jnp.exp(sc - m)
                l = l + jnp.sum(ec, axis=-1, keepdims=True)
                o = o + jnp.dot(ec.astype(BF16), vcs[hh], preferred_element_type=F32)
            o = o / l
            acc = o if acc is None else acc + o
        o_ref[:, psl] = acc.astype(BF16)


def _attn_prompt_call(q, k, v):
    blk = lambda w_: pl.BlockSpec((SEQ, w_), lambda b: (b, 0))
    return pl.pallas_call(
        functools.partial(_attn_kernel, has_ctx=False),
        grid=(BATCH,),
        in_specs=[blk(1024), blk(1024), blk(512)],
        out_specs=blk(512),
        out_shape=jax.ShapeDtypeStruct((NP_ROWS, 512), BF16),
        compiler_params=_cparams("parallel"),
        name="attn_prompt",
    )(q, k, v)


def _attn_sample_call(q, k, v, kc, vc):
    nq = DEC_SEQ // TQ
    qoff = NP_ROWS // TQ
    return pl.pallas_call(
        functools.partial(_attn_kernel, has_ctx=True),
        grid=(DEC_BATCH, nq),
        in_specs=[pl.BlockSpec((TQ, 1024), lambda b, i: (qoff + b * nq + i, 0)),
                  pl.BlockSpec((DEC_SEQ, 1024), lambda b, i: (1 + b, 0)),
                  pl.BlockSpec((DEC_SEQ, 512), lambda b, i: (1 + b, 0)),
                  pl.BlockSpec((PAST_LEN, 1024), lambda b, i: (b, 0)),
                  pl.BlockSpec((PAST_LEN, 512), lambda b, i: (b, 0))],
        out_specs=pl.BlockSpec((TQ, 512), lambda b, i: (b * nq + i, 0)),
        out_shape=jax.ShapeDtypeStruct((NS_ROWS, 512), BF16),
        compiler_params=_cparams("parallel", "parallel"),
        name="attn_sample",
    )(q, k, v, kc, vc)


def _segsum(y2, bd_ref):
    hi = y2.astype(BF16)
    lo = (y2 - hi.astype(F32)).astype(BF16)
    return (jnp.dot(hi, bd_ref[...], preferred_element_type=F32)
            + jnp.dot(lo, bd_ref[...], preferred_element_type=F32))


def _gdn_prep_kernel(x_ref, xp_ref, xn_ref, cw_ref, bd_ref, o_ref, *, tiles_per_seq_sample, prompt_tiles):
    i = pl.program_id(0)
    x = x_ref[...]
    tb = x.shape[0]
    in_sample = i >= prompt_tiles
    pos = (i - prompt_tiles) % tiles_per_seq_sample
    first = jnp.logical_or(jnp.logical_not(in_sample), pos == 0)
    last = jnp.logical_or(jnp.logical_not(in_sample), pos == tiles_per_seq_sample - 1)
    prev_row = xp_ref[7:8, :] * jnp.where(first, 0.0, 1.0)
    next_row = xn_ref[0:1, :] * jnp.where(last, 0.0, 1.0)
    ri = lax.broadcasted_iota(jnp.int32, x.shape, 0)
    xprev = jnp.where(ri == 0, prev_row, pltpu.roll(x, 1, 0))
    xnext = jnp.where(ri == tb - 1, next_row, pltpu.roll(x, tb - 1, 0))
    y = _silu(xprev * cw_ref[0:1, :] + x * cw_ref[1:2, :] + xnext * cw_ref[2:3, :])
    q = y[:, 0:512]
    k = y[:, 512:1024]
    o_ref[:, 0:512] = q * lax.rsqrt(_segsum(q * q, bd_ref) + EPS) * (GDN_DK ** -0.5)
    o_ref[:, 512:1024] = k * lax.rsqrt(_segsum(k * k, bd_ref) + EPS)
    o_ref[:, 1024:1536] = y[:, 1024:1536]


def _gdn_prep_call(gqkv, conv_w8, bd):
    tb = SCAN_ROWS
    n = T_ALL // tb
    r8 = tb // 8
    nb8 = T_ALL // 8
    return pl.pallas_call(
        functools.partial(_gdn_prep_kernel, tiles_per_seq_sample=DEC_SEQ // tb, prompt_tiles=NP_ROWS // tb),
        grid=(n,),
        in_specs=[pl.BlockSpec((tb, 1536), lambda i: (i, 0)),
                  pl.BlockSpec((8, 1536), lambda i: (jnp.maximum(i * r8 - 1, 0), 0)),
                  pl.BlockSpec((8, 1536), lambda i: (jnp.minimum((i + 1) * r8, nb8 - 1), 0)),
                  _full((8, 1536)), _full((512, 512))],
        out_specs=pl.BlockSpec((tb, 1536), lambda i: (i, 0)),
        out_shape=jax.ShapeDtypeStruct((T_ALL, 1536), F32),
        compiler_params=_cparams("parallel"),
        name="gdn_prep",
    )(gqkv, gqkv, gqkv, conv_w8, bd)


def _pair_masks(reverse):
    ri = lax.broadcasted_iota(jnp.int32, (CH, LANES), 0)
    li = lax.broadcasted_iota(jnp.int32, (CH, LANES), 1)
    jm = jnp.bitwise_and(li, HALF - 1)
    tri = (ri <= jm) if reverse else (ri >= jm)
    strict = (ri < jm) if reverse else (ri > jm)
    diag = ri == jm
    left = li < HALF
    r2 = lax.broadcasted_iota(jnp.int32, (2 * CH, LANES), 0)
    l2 = lax.broadcasted_iota(jnp.int32, (2 * CH, LANES), 1)
    bdm = (r2 < HALF) == (l2 < HALF)
    return tri, strict, diag, left, bdm


def _bd(x, bdm):
    return jnp.where(bdm, jnp.concatenate([x, x], axis=0), 0.0)


def _row_form(col, diag):
    return jnp.sum(jnp.where(diag, col, 0.0), axis=0, keepdims=True)


def _gdn_scan_kernel(x_ref, g_ref, s0_ref, sel_ref, tri_ref, o_ref, sfin_ref, s_scr, *, reverse, n_chunks):
    c_idx = pl.program_id(1)

    @pl.when(c_idx == 0)
    def _():
        s_scr[...] = s0_ref[...]

    tri, strict, diag, _, bdm = _pair_masks(reverse)
    eye = jnp.where(diag, 1.0, 0.0)
    last = 0 if reverse else CH - 1
    ri = lax.broadcasted_iota(jnp.int32, (CH, LANES), 0)
    jm = jnp.bitwise_and(lax.broadcasted_iota(jnp.int32, (CH, LANES), 1), HALF - 1)
    same = [jnp.right_shift(ri, s) == jnp.right_shift(jm, s) for s in (3, 4, 5)]
    same8 = same[0]
    offs = (jnp.logical_and(same[1], jnp.logical_not(same[0])),
            jnp.logical_and(same[2], jnp.logical_not(same[1])),
            jnp.logical_not(same[2]))

    npair = GDN_HEADS // 2
    g = g_ref[...]
    gcs = _dot_x01(_dot_01x(tri_ref[...], g), sel_ref[:, 0:512])
    bex = _dot_x01(g, sel_ref[:, 512:1024])
    chains = [(c, p) for c in range(n_chunks) for p in range(npair)]

    st = {}
    for c, p in chains:
        rows = slice(c * CH, (c + 1) * CH)
        psl = slice(p * LANES, (p + 1) * LANES)
        q2 = x_ref[rows, p * LANES:(p + 1) * LANES]
        k2 = x_ref[rows, 512 + p * LANES:512 + (p + 1) * LANES]
        v2 = x_ref[rows, 1024 + p * LANES:1024 + (p + 1) * LANES]
        gcol = gcs[rows, psl]
        bcol = bex[rows, psl]
        grow = _row_form(gcol, diag)
        dec = jnp.exp(jnp.where(tri, gcol - grow, -jnp.inf))
        kb = k2 * bcol
        eg = jnp.exp(gcol)
        glrow = gcol[last:last + 1, :]
        ktbd = _bd(k2, bdm).T
        st[c, p] = dict(dec=dec, ktbd=ktbd, qd=q2 * eg, glrow=glrow,
                        lhs=jnp.concatenate([kb, q2], axis=0),
                        rhs=jnp.concatenate([_bd(v2 * bcol, bdm), _bd(kb * eg, bdm)], axis=1),
                        ktdec=ktbd * jnp.exp(glrow - grow))
    for key in chains:
        d = st[key]
        aq = _dot(d.pop("lhs"), d["ktbd"])
        dec = d.pop("dec")
        d["a"] = jnp.where(strict, aq[0:CH] * dec, 0.0)
        d["attn"] = aq[CH:2 * CH] * dec

    def bd2(parts):
        return tuple(jnp.where(bdm, jnp.concatenate([x, x], axis=0), jnp.zeros((), BF16)) for x in parts)

    for key in chains:
        d = st[key]
        b = -jnp.where(same8, d["a"], 0.0)
        d["t"] = eye + b
        d["ps"] = _split2(b)
        d["pb"] = bd2(d["ps"])
    for key in chains:
        d = st[key]
        d["ps"] = _split2(_dot_x3(d["ps"], d["pb"]))
        d["pb"] = bd2(d["ps"])
    for key in chains:
        d = st[key]
        d["t"] = d["t"] + _dot_x3(_split2(d["t"]), d["pb"])
        d["pb"] = bd2(_split2(_dot_x3(d.pop("ps"), d["pb"])))
    for key in chains:
        d = st[key]
        d["t"] = d["t"] + _dot_x3(_split2(d["t"]), d.pop("pb"))
    for off in offs:
        for key in chains:
            d = st[key]
            d["y"] = _dot(d["t"], _bd(jnp.where(off, d["a"], 0.0), bdm))
        for key in chains:
            d = st[key]
            d["t"] = d["t"] - _dot(d.pop("y"), _bd(d["t"], bdm))
    for key in chains:
        d = st[key]
        d.pop("a")
        d["uw"] = _dot(d.pop("t"), d.pop("rhs"))

    s = [s_scr[p] for p in range(npair)]
    for ci in range(n_chunks):
        c = (n_chunks - 1 - ci) if reverse else ci
        rows = slice(c * CH, (c + 1) * CH)
        wq = [_dot(jnp.concatenate([st[c, p]["uw"][:, LANES:2 * LANES], st[c, p]["qd"]], axis=0), s[p])
              for p in range(npair)]
        for p in range(npair):
            d = st[c, p]
            vnbd = _bd(d["uw"][:, 0:LANES] - wq[p][0:CH], bdm)
            o_ref[rows, p * LANES:(p + 1) * LANES] = wq[p][CH:2 * CH] + _dot(d["attn"], vnbd)
            s[p] = s[p] * jnp.exp(d["glrow"]) + _dot(d["ktdec"], vnbd)
    for p in range(npair):
        s_scr[p] = s[p]

    @pl.when(c_idx == pl.num_programs(1) - 1)
    def _():
        sfin_ref[...] = s_scr[...]


def _gdn_scan_call(xq, gates, s0, sel, tri, *, n_seq, seq_len, row_off, reverse):
    nblk = seq_len // SCAN_ROWS
    boff = row_off // SCAN_ROWS

    def rmap(b, j):
        jj = (nblk - 1 - j) if reverse else j
        return (boff + b * nblk + jj, 0)

    def omap(b, j):
        jj = (nblk - 1 - j) if reverse else j
        return (b * nblk + jj, 0)

    return pl.pallas_call(
        functools.partial(_gdn_scan_kernel, reverse=reverse, n_chunks=SCAN_ROWS // CH),
        grid=(n_seq, nblk),
        in_specs=[pl.BlockSpec((SCAN_ROWS, 1536), rmap),
                  pl.BlockSpec((SCAN_ROWS, LANES), rmap),
                  pl.BlockSpec((None, 4, LANES, LANES), lambda b, j: (b, 0, 0, 0)),
                  _full(sel.shape), _full((SCAN_ROWS, SCAN_ROWS))],
        out_specs=[pl.BlockSpec((SCAN_ROWS, 512), omap),
                   pl.BlockSpec((None, 4, LANES, LANES), lambda b, j: (b, 0, 0, 0))],
        out_shape=[jax.ShapeDtypeStruct((n_seq * seq_len, 512), F32),
                   jax.ShapeDtypeStruct((n_seq, 4, LANES, LANES), F32)],
        scratch_shapes=[pltpu.VMEM((4, LANES, LANES), F32)],
        compiler_params=_cparams("parallel", "arbitrary"),
        name="gdn_scan_bwd" if reverse else "gdn_scan_fwd",
    )(xq, gates, s0, sel, tri)


def _mlstm_scan_kernel(q_ref, k_ref, v_ref, g_ref, c0_ref, m0_ref, sel_ref, tri_ref,
                       o_ref, cfin_ref, mfin_ref, c_scr, m_scr, *, reverse, n_chunks):
    c_idx = pl.program_id(1)

    @pl.when(c_idx == 0)
    def _():
        c_scr[...] = c0_ref[...]
        m_scr[...] = m0_ref[...]

    tri, _, diag, left, bdm = _pair_masks(reverse)
    left1 = left[0:1, :]
    last = 0 if reverse else CH - 1
    ninf = -jnp.inf
    ones = jnp.ones((CH, LANES), F32)
    rtop = lax.broadcasted_iota(jnp.int32, (2 * CH, 2 * LANES), 0) < HALF

    npair = MLSTM_HEADS // 2
    right = jnp.logical_not(left)
    g = g_ref[...]
    ex = _dot_x01(g, sel_ref[:, 0:1024])
    bfull = _dot_x01(_dot_01x(tri_ref[...], g), sel_ref[:, 1024:2048])
    chains = [(c, p) for c in range(n_chunks) for p in range(npair)]
    two = lambda a: jnp.concatenate([a, a], axis=1)

    st = {}
    for c, p in chains:
        rows = slice(c * CH, (c + 1) * CH)
        h0, h1 = 2 * p, 2 * p + 1
        psl = slice(p * LANES, (p + 1) * LANES)
        b0 = bfull[rows, h0 * LANES:(h0 + 1) * LANES]
        b1 = bfull[rows, h1 * LANES:(h1 + 1) * LANES]
        bcol = jnp.where(left, b0, b1)
        icol = jnp.where(left, ex[rows, h0 * LANES:(h0 + 1) * LANES], ex[rows, h1 * LANES:(h1 + 1) * LANES])
        brow = _row_form(bcol, diag)
        irow = _row_form(icol, diag)
        dlog = jnp.where(tri, bcol - brow + irow, ninf)
        dmax0 = jnp.max(jnp.where(left, dlog, ninf), axis=-1, keepdims=True)
        dmax1 = jnp.max(jnp.where(left, ninf, dlog), axis=-1, keepdims=True)
        bl0 = b0[last:last + 1, :]
        bl1 = b1[last:last + 1, :]
        glog = jnp.where(left1, bl0, bl1) - brow + irow
        gmax0 = jnp.max(jnp.where(left1, glog, ninf), axis=-1, keepdims=True)
        gmax1 = jnp.max(jnp.where(left1, ninf, glog), axis=-1, keepdims=True)
        q2 = q_ref[rows, psl]
        ktbd = _bd(k_ref[rows, psl], bdm).T
        vx = jnp.concatenate(
            [jnp.concatenate([v_ref[rows, h0 * LANES:(h0 + 1) * LANES], ones], axis=1),
             jnp.concatenate([v_ref[rows, h1 * LANES:(h1 + 1) * LANES], ones], axis=1)], axis=0)
        st[c, p] = dict(b=(b0, b1), dmax=(dmax0, dmax1), bl=(bl0, bl1), gmax=(gmax0, gmax1), q2=q2, ktbd=ktbd,
                        vxb=vx.astype(BF16), ew=jnp.exp(dlog - jnp.where(left, dmax0, dmax1)),
                        kws=ktbd * jnp.exp(glog - jnp.where(left1, gmax0, gmax1)))
    for key in chains:
        d = st[key]
        d["wts"] = d.pop("ew") * _dot(d["q2"], d.pop("ktbd"))
    for key in chains:
        d = st[key]
        wts = d.pop("wts")
        d["y"] = tuple(jnp.dot(jnp.where(keep, wts, 0.0).astype(BF16), d["vxb"], preferred_element_type=F32)
                       for keep in (left, right))
        d["upd"] = jnp.dot(d.pop("kws").astype(BF16), d.pop("vxb"), preferred_element_type=F32)

    cx = [c_scr[p] for p in range(npair)]
    m = [m_scr[h:h + 1, :] for h in range(MLSTM_HEADS)]
    for ci in range(n_chunks):
        c = (n_chunks - 1 - ci) if reverse else ci
        rows = slice(c * CH, (c + 1) * CH)
        for p in range(npair):
            d = st[c, p]
            cxb = cx[p].astype(BF16)
            dec, sc = [], []
            for hh, keep in enumerate((left, right)):
                hd = 2 * p + hh
                inter = d["b"][hh] + m[hd]
                mt = jnp.maximum(inter, d["dmax"][hh])
                z = jnp.dot(jnp.where(keep, d["q2"], 0.0).astype(BF16), cxb, preferred_element_type=F32)
                xo = d["y"][hh] * two(jnp.exp(d["dmax"][hh] - mt)) + two(jnp.exp(inter - mt)) * z
                den = jnp.maximum(jnp.abs(xo[:, LANES:2 * LANES]), jnp.exp(-mt))
                o_ref[rows, hd * LANES:(hd + 1) * LANES] = xo[:, 0:LANES] / den
                mn = jnp.maximum(d["bl"][hh] + m[hd], d["gmax"][hh])
                dec.append(two(jnp.exp(d["bl"][hh] + m[hd] - mn)))
                sc.append(two(jnp.exp(d["gmax"][hh] - mn)))
                m[hd] = mn
            cx[p] = jnp.where(rtop, dec[0], dec[1]) * cx[p] + jnp.where(rtop, sc[0], sc[1]) * d["upd"]
    for p in range(npair):
        c_scr[p] = cx[p]
    for h in range(MLSTM_HEADS):
        m_scr[h:h + 1, :] = m[h]

    @pl.when(c_idx == pl.num_programs(1) - 1)
    def _():
        cfin_ref[...] = c_scr[...]
        mfin_ref[...] = m_scr[...]


def _mlstm_scan_call(q, k, v, gates, c0, m0, sel, tri, *, n_seq, seq_len, row_off, reverse):
    nblk = seq_len // SCAN_ROWS
    boff = row_off // SCAN_ROWS

    def rmap(b, j):
        jj = (nblk - 1 - j) if reverse else j
        return (boff + b * nblk + jj, 0)

    def omap(b, j):
        jj = (nblk - 1 - j) if reverse else j
        return (b * nblk + jj, 0)

    st = lambda *shape: pl.BlockSpec((None,) + shape, lambda b, j: (b,) + (0,) * len(shape))
    return pl.pallas_call(
        functools.partial(_mlstm_scan_kernel, reverse=reverse, n_chunks=SCAN_ROWS // CH),
        grid=(n_seq, nblk),
        in_specs=[pl.BlockSpec((SCAN_ROWS, 512), rmap), pl.BlockSpec((SCAN_ROWS, 512), rmap),
                  pl.BlockSpec((SCAN_ROWS, 1024), rmap), pl.BlockSpec((SCAN_ROWS, LANES), rmap),
                  st(4, LANES, 2 * LANES), st(8, LANES), _full(sel.shape), _full((SCAN_ROWS, SCAN_ROWS))],
        out_specs=[pl.BlockSpec((SCAN_ROWS, 1024), omap), st(4, LANES, 2 * LANES), st(8, LANES)],
        out_shape=[jax.ShapeDtypeStruct((n_seq * seq_len, 1024), F32),
                   jax.ShapeDtypeStruct((n_seq, 4, LANES, 2 * LANES), F32),
                   jax.ShapeDtypeStruct((n_seq, 8, LANES), F32)],
        scratch_shapes=[pltpu.VMEM((4, LANES, 2 * LANES), F32), pltpu.VMEM((8, LANES), F32)],
        compiler_params=_cparams("parallel", "arbitrary"),
        name="mlstm_scan_bwd" if reverse else "mlstm_scan_fwd",
    )(q, k, v, gates, c0, m0, sel, tri)


def _odd_in_kernel(x_ref, m_ref, nw_ref, w_ref, gb_ref, q_ref, k_ref, v_ref, og_ref, gate_ref):
    h = _norm_mod(x_ref[...], nw_ref[...], m_ref[1:2, :], m_ref[0:1, :])
    proj = jnp.dot(h.astype(BF16), w_ref[...], preferred_element_type=F32)
    q_ref[...] = proj[:, 0:512]
    k_ref[...] = proj[:, 512:1024] * (MLSTM_DK ** -0.5)
    v_ref[...] = proj[:, 1024:2048]
    og_ref[...] = proj[:, 2048:3072]
    g = proj[:, 3072:3200] + gb_ref[...]
    lane = lax.broadcasted_iota(jnp.int32, g.shape, 1)
    gate_ref[...] = jnp.where(lane < 16, g, -_softplus(-g))


def _odd_in_call(x, mod3, nw, w, gbias):
    tm = TM_IN
    nt = GROUP // tm
    row = lambda w_: pl.BlockSpec((tm, w_), lambda i: (i, 0))
    outs = [512, 512, 1024, 1024, 128]
    return pl.pallas_call(
        _odd_in_kernel,
        grid=(T_ALL // tm,),
        in_specs=[row(D_MODEL), pl.BlockSpec((None, 8, D_MODEL), lambda i: (i // nt, 0, 0)),
                  _full((1, D_MODEL)), _full(w.shape), _full((1, LANES))],
        out_specs=[row(w_) for w_ in outs],
        out_shape=[jax.ShapeDtypeStruct((T_ALL, w_), F32) for w_ in outs],
        compiler_params=_cparams("parallel"),
        name="odd_in",
    )(x, mod3, nw, w, gbias)


def _ffn_kernel(*refs, even):
    if even:
        (x_ref, m_ref, nw_ref, attn_ref, of_ref, ob_ref, gz_ref, onw_ref, bd_ref, wo_ref, win_ref, wout_ref,
         o_ref, x1_scr, hn_scr, acc_scr) = refs
    else:
        (x_ref, m_ref, nw_ref, hf_ref, hb_ref, og_ref, onw_ref, wo_ref, win_ref, wout_ref,
         o_ref, x1_scr, hn_scr, acc_scr) = refs
    j = pl.program_id(1)

    @pl.when(j == 0)
    def _():
        if even:
            d = of_ref[...] + ob_ref[...]
            ms = _segsum(d * d, bd_ref) * (1.0 / GDN_DV)
            dn = d * lax.rsqrt(ms + EPS) * onw_ref[...] * _silu(gz_ref[...])
            mixed = (jnp.dot(attn_ref[...], wo_ref[0:512, :], preferred_element_type=F32)
                     + jnp.dot(dn.astype(BF16), wo_ref[512:1024, :], preferred_element_type=F32))
        else:
            d = hf_ref[...] + hb_ref[...]
            parts = []
            for h in range(MLSTM_HEADS):
                dh = d[:, h * LANES:(h + 1) * LANES]
                parts.append(dh * lax.rsqrt(jnp.mean(dh * dh, axis=-1, keepdims=True) + EPS))
            mem = jnp.concatenate(parts, axis=1) * onw_ref[...] * jax.nn.sigmoid(og_ref[...])
            mixed = jnp.dot(mem.astype(BF16), wo_ref[...], preferred_element_type=F32)
        x1 = x_ref[...] + m_ref[2:3, :] * mixed
        x1_scr[...] = x1
        hn_scr[...] = _norm_mod(x1, nw_ref[...], m_ref[4:5, :], m_ref[3:4, :]).astype(BF16)
        acc_scr[...] = jnp.zeros_like(acc_scr)

    gu = jnp.dot(hn_scr[...], win_ref[...], preferred_element_type=F32)
    act = _silu(gu[:, 0:TF]) * gu[:, TF:2 * TF]
    acc_scr[...] += jnp.dot(act.astype(BF16), wout_ref[...], preferred_element_type=F32)

    @pl.when(j == pl.num_programs(1) - 1)
    def _():
        o_ref[...] = x1_scr[...] + m_ref[5:6, :] * acc_scr[...]


def _ffn_call(even, x, mod3, nw, mix_inputs, onw, bd, wo, win, wout):
    tm = TM_FFN
    nt = GROUP // tm
    nf = D_FF // TF
    row = lambda w_: pl.BlockSpec((tm, w_), lambda i, j: (i, 0))
    const = lambda shape: pl.BlockSpec(shape, lambda i, j: (0,) * len(shape))
    in_specs = [row(D_MODEL), pl.BlockSpec((None, 8, D_MODEL), lambda i, j: (i // nt, 0, 0)), const((1, D_MODEL))]
    in_specs += [row(a.shape[1]) for a in mix_inputs]
    args = [x, mod3, nw] + list(mix_inputs) + [onw]
    in_specs.append(const(onw.shape))
    if even:
        in_specs.append(const(bd.shape))
        args.append(bd)
    in_specs += [const(wo.shape),
                 pl.BlockSpec((None, D_MODEL, 2 * TF), lambda i, j: (j, 0, 0)),
                 pl.BlockSpec((TF, D_MODEL), lambda i, j: (j, 0))]
    args += [wo, win, wout]
    return pl.pallas_call(
        functools.partial(_ffn_kernel, even=even),
        grid=(T_ALL // tm, nf),
        in_specs=in_specs,
        out_specs=row(D_MODEL),
        out_shape=jax.ShapeDtypeStruct((T_ALL, D_MODEL), F32),
        scratch_shapes=[pltpu.VMEM((tm, D_MODEL), F32), pltpu.VMEM((tm, D_MODEL), BF16),
                        pltpu.VMEM((tm, D_MODEL), F32)],
        compiler_params=_cparams("parallel", "arbitrary"),
        name="ffn_even" if even else "ffn_odd",
    )(*args)


def _rope_tables():
    rows = DEC_SEQ // GRID_W
    row = jnp.repeat(jnp.arange(rows, dtype=F32), GRID_W)
    col = jnp.tile(jnp.arange(GRID_W, dtype=F32), rows)
    per_axis = MLA_ROPE // 2
    inv = 1.0 / (ROPE_BASE ** (jnp.arange(0, per_axis, 2, dtype=F32) / per_axis))
    ar = row[:, None] * inv
    ac = col[:, None] * inv
    ang = jnp.concatenate([ar, ar, ac, ac], axis=-1)
    cos, sin = jnp.cos(ang), jnp.sin(ang)
    first = np.tile(np.concatenate([np.ones(8), np.zeros(8)]), 2).astype(np.float32)
    pad = lambda a, fill: jnp.concatenate(
        [jnp.full((DEC_SEQ, MLA_NOPE), fill, F32), a, jnp.full((DEC_SEQ, LANES - MLA_QK), fill, F32)], axis=1)
    cos_t = pad(cos, 1.0)
    sa_t = pad(-sin * first, 0.0)
    sb_t = pad(sin * (1.0 - first), 0.0)
    ident = lambda fill: jnp.full((GROUP, LANES), fill, F32)
    return (jnp.concatenate([ident(1.0), cos_t]), jnp.concatenate([ident(0.0), sa_t]),
            jnp.concatenate([ident(0.0), sb_t]))


def _pad_cols(a, left, total):
    return jnp.pad(a, ((0, 0), (left, total - left - a.shape[1])))


def _lane_row(vec, total=LANES):
    return jnp.pad(vec.reshape(1, -1).astype(F32), ((0, 0), (0, total - vec.size)))


def _gdn_sel(direction):
    j = np.arange(512)
    head = 2 * (j // LANES) + (j % LANES >= HALF)
    sel = np.zeros((LANES, 1024), np.float32)
    sel[direction * 8 + head, j] = 1.0
    sel[16 + direction * 8 + head, 512 + j] = 1.0
    return jnp.asarray(sel, dtype=BF16)


def _mlstm_sel(direction):
    j = np.arange(1024)
    sel = np.zeros((LANES, 2048), np.float32)
    sel[direction * 8 + j // LANES, j] = 1.0
    sel[16 + direction * 8 + j // LANES, 1024 + j] = 1.0
    return jnp.asarray(sel, dtype=BF16)


def _tri_blocks(reverse):
    t = np.tril(np.ones((CH, CH), np.float32))
    return jnp.asarray(np.kron(np.eye(SCAN_ROWS // CH, dtype=np.float32), t.T if reverse else t), dtype=BF16)


def _block_diag_pairs(s):
    lead = s.shape[:-3]
    s = s.reshape(lead + (4, 2, HALF, HALF))
    z = jnp.zeros_like(s[..., 0, :, :])
    top = jnp.concatenate([s[..., 0, :, :], z], axis=-1)
    bot = jnp.concatenate([z, s[..., 1, :, :]], axis=-1)
    return jnp.concatenate([top, bot], axis=-2)


def _unblock_diag_pairs(s):
    a = s[..., :HALF, :HALF]
    b = s[..., HALF:, HALF:]
    return jnp.stack([a, b], axis=-3).reshape(s.shape[:-3] + (8, HALF, HALF))


def kernel(x_prompt, x_sample, c, c_ctx, cache_mla_ckv, cache_mla_krope, state_gdn, state_mlstm_C, state_mlstm_n,
           state_mlstm_m, norm_mix, norm_ffn, w_ada, b_ada, w_ffn_in, w_ffn_out, w_even_in, w_even_out,
           mla_q_a_norm, mla_kv_a_norm, w_mla_uq, w_mla_ukv, mla_q_norm, mla_k_norm, gdn_conv, gdn_a_log,
           gdn_dt_bias, gdn_out_norm, w_odd_in, w_odd_out, mlstm_gate_bias, mlstm_out_norm):
    x = jnp.concatenate([x_prompt.reshape(NP_ROWS, D_MODEL), x_sample.reshape(NS_ROWS, D_MODEL)], axis=0)

    cond8 = jnp.concatenate([c_ctx[None, :], c, jnp.zeros((8 - 1 - DEC_BATCH, D_MODEL), F32)], axis=0)
    mods = _ada_call(cond8, w_ada, b_ada)
    mods = mods[:, :3].reshape(DEPTH, 3, 6, D_MODEL)
    mods = jnp.pad(mods, ((0, 0), (0, 0), (0, 2), (0, 0)))

    cos_t, sa_t, sb_t = _rope_tables()
    bd512 = jnp.asarray(np.kron(np.eye(8, dtype=np.float32), np.ones((HALF, HALF), np.float32))).astype(BF16)
    trib_f, trib_b = _tri_blocks(False), _tri_blocks(True)

    ckv_new, krope_new, gdn_new, mc_new, mn_new, mm_new = [], [], [], [], [], []
    for l in range(DEPTH):
        j = l // 2
        mod3 = mods[l]
        win_l = w_ffn_in[l].astype(BF16)
        win_t = jnp.concatenate([win_l[:, :D_FF].reshape(D_MODEL, D_FF // TF, TF),
                                 win_l[:, D_FF:].reshape(D_MODEL, D_FF // TF, TF)], axis=-1).transpose(1, 0, 2)
        wout_l = w_ffn_out[l].astype(BF16)
        nw1 = norm_mix[l].reshape(1, D_MODEL)
        nw2 = norm_ffn[l].reshape(1, D_MODEL)
        if l % 2 == 0:
            wi = w_even_in[j].astype(BF16)
            w_arr = jnp.concatenate([wi[:, 0:640], _pad_cols(wi[:, 640:672], MLA_NOPE, LANES), wi[:, 672:2720],
                                     _pad_cols(wi[:, 2720:2752], 0, LANES)], axis=1)
            wuq = jnp.pad(w_mla_uq[j].astype(BF16).reshape(Q_LORA, MLA_HEADS, MLA_QK),
                          ((0, 0), (0, 0), (0, LANES - MLA_QK))).reshape(Q_LORA, MLA_HEADS * LANES)
            wukv = w_mla_ukv[j].astype(BF16).reshape(KV_LORA, MLA_HEADS, MLA_NOPE + MLA_V)
            wk = jnp.pad(wukv[:, :, :MLA_NOPE], ((0, 0), (0, 0), (0, LANES - MLA_NOPE))).reshape(KV_LORA, -1)
            wv = wukv[:, :, MLA_NOPE:].reshape(KV_LORA, MLA_HEADS * MLA_V)
            qn = _lane_row(mla_q_norm[j])
            kn = _lane_row(mla_k_norm[j])
            gp = jnp.concatenate([_lane_row(gdn_a_log[j]), _lane_row(gdn_dt_bias[j]), jnp.zeros((6, LANES), F32)])
            q, k, v, ckvn, kr, gqkv, gz, gates = _even_in_call(
                x, mod3, nw1, w_arr, mla_q_a_norm[j].reshape(1, -1), mla_kv_a_norm[j].reshape(1, -1),
                wuq, wk, wv, qn, kn, cos_t, sa_t, sb_t, gp)
            ckv_new.append(ckvn[:NP_ROWS].reshape(BATCH, SEQ, KV_LORA))
            krope_new.append(kr[:NP_ROWS, MLA_NOPE:MLA_QK].reshape(BATCH, SEQ, MLA_ROPE))
            kc, vc = _ctx_kv_call(cache_mla_ckv[:, j].reshape(DEC_BATCH * PAST_LEN, KV_LORA),
                                  _pad_cols(cache_mla_krope[:, j].reshape(DEC_BATCH * PAST_LEN, MLA_ROPE),
                                            MLA_NOPE, LANES), wk, wv, kn)
            attn = jnp.concatenate([_attn_prompt_call(q, k, v), _attn_sample_call(q, k, v, kc, vc)], axis=0)
            conv8 = jnp.pad(gdn_conv[j], ((0, 5), (0, 0)))
            xq = _gdn_prep_call(gqkv, conv8, bd512)
            zeros_p = jnp.zeros((BATCH, 4, LANES, LANES), F32)
            o_dirs, s_dirs = [], []
            for d, (rev, tri_d) in enumerate(((False, trib_f), (True, trib_b))):
                sel = _gdn_sel(d)
                s0 = _block_diag_pairs(state_gdn[:, j, d])
                op, sp = _gdn_scan_call(xq, gates, zeros_p, sel, tri_d, n_seq=BATCH, seq_len=SEQ, row_off=0,
                                        reverse=rev)
                os_, _ = _gdn_scan_call(xq, gates, s0, sel, tri_d, n_seq=DEC_BATCH, seq_len=DEC_SEQ,
                                        row_off=NP_ROWS, reverse=rev)
                o_dirs.append(jnp.concatenate([op, os_], axis=0))
                s_dirs.append(_unblock_diag_pairs(sp))
            gdn_new.append(jnp.stack(s_dirs, axis=1))
            onw = jnp.tile(gdn_out_norm[j], GDN_HEADS).reshape(1, -1)
            x = _ffn_call(True, x, mod3, nw2, [attn, o_dirs[0], o_dirs[1], gz], onw, bd512,
                          w_even_out[j].astype(BF16), win_t, wout_l)
        else:
            wi = w_odd_in[j].astype(BF16)
            w_arr = jnp.concatenate([wi[:, :3072], _pad_cols(wi[:, 3072:3104], 0, LANES)], axis=1)
            q, k, v, og, gates = _odd_in_call(x, mod3, nw1, w_arr, _lane_row(mlstm_gate_bias[j]))
            h_dirs, c_dirs, m_dirs = [], [], []
            for d, (rev, tri_d) in enumerate(((False, trib_f), (True, trib_b))):
                sel = _mlstm_sel(d)
                c0 = state_mlstm_C[:, j, d]
                n0 = state_mlstm_n[:, j, d]
                cx0 = jnp.concatenate([c0, jnp.broadcast_to(n0[..., None], c0.shape)], axis=-1)
                cx0 = cx0.reshape(DEC_BATCH, 4, LANES, 2 * LANES)
                m0 = jnp.broadcast_to(state_mlstm_m[:, j, d][..., None], (DEC_BATCH, 8, LANES))
                hp, cp, mp = _mlstm_scan_call(q, k, v, gates, jnp.zeros((BATCH, 4, LANES, 2 * LANES), F32),
                                              jnp.zeros((BATCH, 8, LANES), F32), sel, tri_d,
                                              n_seq=BATCH, seq_len=SEQ, row_off=0, reverse=rev)
                hs, _, _ = _mlstm_scan_call(q, k, v, gates, cx0, m0, sel, tri_d,
                                            n_seq=DEC_BATCH, seq_len=DEC_SEQ, row_off=NP_ROWS, reverse=rev)
                h_dirs.append(jnp.concatenate([hp, hs], axis=0))
                c_dirs.append(cp.reshape(BATCH, 8, HALF, 2 * LANES))
                m_dirs.append(mp[:, :, 0])
            cst = jnp.stack(c_dirs, axis=1)
            mc_new.append(cst[..., :LANES])
            mn_new.append(cst[..., LANES])
            mm_new.append(jnp.stack(m_dirs, axis=1))
            x = _ffn_call(False, x, mod3, nw2, [h_dirs[0], h_dirs[1], og], mlstm_out_norm[j].reshape(1, -1), None,
                          w_odd_out[j].astype(BF16), win_t, wout_l)

    return (x[:NP_ROWS].reshape(BATCH, SEQ, D_MODEL), x[NP_ROWS:].reshape(DEC_BATCH, DEC_SEQ, D_MODEL),
            jnp.stack(ckv_new, axis=1), jnp.stack(krope_new, axis=1), jnp.stack(gdn_new, axis=1),
            jnp.stack(mc_new, axis=1), jnp.stack(mn_new, axis=1), jnp.stack(mm_new, axis=1))
```

```python
import functools

import jax
import jax.numpy as jnp
import numpy as np
from jax import lax
from jax.experimental import pallas as pl
from jax.experimental.pallas import tpu as pltpu

F32 = jnp.float32
BF16 = jnp.bfloat16
HIGHEST = lax.Precision.HIGHEST

D_MODEL = 1024
BATCH = 16
SEQ = 256
DEPTH = 4
DEC_BATCH = 2
DEC_SEQ = 4096
PAST_LEN = 256
GRID_W = 64
EPS = 1e-6
ROPE_BASE = 10000.0
MLA_HEADS = 8
MLA_NOPE = 64
MLA_ROPE = 32
MLA_QK = MLA_NOPE + MLA_ROPE
MLA_V = 64
Q_LORA = 384
KV_LORA = 256
GDN_HEADS = 8
GDN_DK = 64
GDN_DV = 64
MLSTM_HEADS = 8
MLSTM_DK = 64
MLSTM_DV = 128
D_FF = 2816

LANES = 128
HALF = 64
CH = 64
NP_ROWS = BATCH * SEQ
NS_ROWS = DEC_BATCH * DEC_SEQ
T_ALL = NP_ROWS + NS_ROWS
GROUP = 4096
assert NP_ROWS == GROUP and DEC_SEQ == GROUP

TM_IN = 256
TM_FFN = 512
TF = 256
TQ = 256
SCAN_ROWS = 256
VMEM_LIMIT = 56 * 1024 * 1024

EVEN_W = 384 + 256 + 128 + 1536 + 512 + 128
ODD_W = 512 + 512 + 1024 + 1024 + 128


def _cparams(*sem):
    return pltpu.CompilerParams(dimension_semantics=sem, vmem_limit_bytes=VMEM_LIMIT)


def _silu(x):
    return x * jax.nn.sigmoid(x)


def _softplus(x):
    return jnp.maximum(x, 0.0) + jnp.log(1.0 + jnp.exp(-jnp.abs(x)))


def _dot(a, b):
    return jnp.dot(a.astype(BF16), b.astype(BF16), preferred_element_type=F32)


def _dot_hi(a, b):
    return jnp.dot(a, b, precision=HIGHEST, preferred_element_type=F32)


def _split3(x):
    hi = x.astype(BF16)
    r = x - hi.astype(F32)
    mid = r.astype(BF16)
    lo = (r - mid.astype(F32)).astype(BF16)
    return hi, mid, lo


def _split2(x):
    hi = x.astype(BF16)
    return hi, (x - hi.astype(F32)).astype(BF16)


def _dot_x3(a, b):
    f = lambda x, y: jnp.dot(x, y, preferred_element_type=F32)
    return f(a[0], b[0]) + f(a[0], b[1]) + f(a[1], b[0])


def _dot_x01(x, m01):
    return sum(jnp.dot(part, m01, preferred_element_type=F32) for part in _split3(x))


def _dot_01x(m01, x):
    return sum(jnp.dot(m01, part, preferred_element_type=F32) for part in _split3(x))


def _norm_mod(x, nw, sc, sh):
    ms = jnp.mean(x * x, axis=-1, keepdims=True)
    return (x * lax.rsqrt(ms + EPS) * nw) * (1.0 + sc) + sh


def _ada_kernel(c_ref, w_ref, b_ref, o_ref):
    c = c_ref[...]
    o_ref[...] = _dot(_silu(c), w_ref[...]) + b_ref[...]


def _ada_call(cond8, w_ada, b_ada):
    tn = 1536
    n = w_ada.shape[-1]
    return pl.pallas_call(
        _ada_kernel,
        grid=(DEPTH, n // tn),
        in_specs=[pl.BlockSpec((8, D_MODEL), lambda l, j: (0, 0)),
                  pl.BlockSpec((None, D_MODEL, tn), lambda l, j: (l, 0, j)),
                  pl.BlockSpec((None, 1, tn), lambda l, j: (l, 0, j))],
        out_specs=pl.BlockSpec((None, 8, tn), lambda l, j: (l, 0, j)),
        out_shape=jax.ShapeDtypeStruct((DEPTH, 8, n), F32),
        compiler_params=_cparams("parallel", "parallel"),
        name="ada",
    )(cond8, w_ada, b_ada.reshape(DEPTH, 1, n))


def _rope(x, cos, sa, sb):
    return x * cos + pltpu.roll(x, LANES - 8, 1) * sa + pltpu.roll(x, 8, 1) * sb


def _head_norm_rope(x, w, rope, scale):
    ms = jnp.sum(x * x, axis=-1, keepdims=True) * (1.0 / MLA_QK)
    y = x * lax.rsqrt(ms + EPS) * w
    if rope is not None:
        y = _rope(y, *rope)
    if scale != 1.0:
        y = y * scale
    return y


def _mla_kv(ckvn, kr, wk_ref, wv_ref, kn, rope, k_ref, v_ref):
    cb = ckvn.astype(BF16)
    kraw = jnp.dot(cb, wk_ref[...], preferred_element_type=F32)
    for h in range(MLA_HEADS):
        sl = slice(h * LANES, (h + 1) * LANES)
        k_ref[:, sl] = _head_norm_rope(kraw[:, sl] + kr, kn, rope, 1.0).astype(BF16)
    v_ref[...] = jnp.dot(cb, wv_ref[...], preferred_element_type=F32).astype(BF16)


def _even_in_kernel(x_ref, m_ref, nw_ref, w_ref, qan_ref, kvan_ref, wuq_ref, wk_ref, wv_ref, qn_ref, kn_ref,
                    cos_ref, sa_ref, sb_ref, gp_ref,
                    q_ref, k_ref, v_ref, ckv_ref, kr_ref, gqkv_ref, gz_ref, gate_ref):
    x = x_ref[...]
    h = _norm_mod(x, nw_ref[...], m_ref[1:2, :], m_ref[0:1, :])
    proj = jnp.dot(h.astype(BF16), w_ref[...], preferred_element_type=F32)
    rope = (cos_ref[...], sa_ref[...], sb_ref[...])
    cq = proj[:, 0:384]
    cqn = cq * lax.rsqrt(jnp.mean(cq * cq, axis=-1, keepdims=True) + EPS) * qan_ref[...]
    qraw = jnp.dot(cqn.astype(BF16), wuq_ref[...], preferred_element_type=F32)
    qn = qn_ref[...]
    for hd in range(MLA_HEADS):
        sl = slice(hd * LANES, (hd + 1) * LANES)
        q_ref[:, sl] = _head_norm_rope(qraw[:, sl], qn, rope, MLA_QK ** -0.5).astype(BF16)
    ckv = proj[:, 384:640]
    ckvn = ckv * lax.rsqrt(jnp.mean(ckv * ckv, axis=-1, keepdims=True) + EPS) * kvan_ref[...]
    ckv_ref[...] = ckvn
    kr = proj[:, 640:768]
    kr_ref[...] = kr
    _mla_kv(ckvn, kr, wk_ref, wv_ref, kn_ref[...], rope, k_ref, v_ref)
    gqkv_ref[...] = proj[:, 768:2304]
    gz_ref[...] = proj[:, 2304:2816]
    g = proj[:, 2816:2944]
    lane = lax.broadcasted_iota(jnp.int32, g.shape, 1)
    log_decay = -jnp.exp(gp_ref[0:1, :]) * _softplus(g + gp_ref[1:2, :])
    gate_ref[...] = jnp.where(lane < 16, log_decay, jax.nn.sigmoid(g))


def _full(shape):
    nd = len(shape)
    return pl.BlockSpec(shape, lambda *_: (0,) * nd)


def _even_in_call(x, mod3, nw, w, qan, kvan, wuq, wk, wv, qn, kn, cos, sa, sb, gp):
    tm = TM_IN
    nt = GROUP // tm
    row = lambda w_: pl.BlockSpec((tm, w_), lambda i: (i, 0))
    rope_spec = pl.BlockSpec((tm, LANES), lambda i: (jnp.where(i < nt, i, nt + i % nt), 0))
    outs = [(1024, BF16), (1024, BF16), (512, BF16), (256, F32), (128, F32), (1536, F32), (512, F32), (128, F32)]
    return pl.pallas_call(
        _even_in_kernel,
        grid=(T_ALL // tm,),
        in_specs=[row(D_MODEL),
                  pl.BlockSpec((None, 8, D_MODEL), lambda i: (i // nt, 0, 0)),
                  _full((1, D_MODEL)), _full(w.shape), _full((1, 384)), _full((1, 256)),
                  _full(wuq.shape), _full(wk.shape), _full(wv.shape), _full((1, LANES)), _full((1, LANES)),
                  rope_spec, rope_spec, rope_spec, _full((8, LANES))],
        out_specs=[row(w_) for w_, _ in outs],
        out_shape=[jax.ShapeDtypeStruct((T_ALL, w_), dt) for w_, dt in outs],
        compiler_params=_cparams("parallel"),
        name="even_in",
    )(x, mod3, nw, w, qan, kvan, wuq, wk, wv, qn, kn, cos, sa, sb, gp)


def _ctx_kv_kernel(ckv_ref, kr_ref, wk_ref, wv_ref, kn_ref, k_ref, v_ref):
    _mla_kv(ckv_ref[...], kr_ref[...], wk_ref, wv_ref, kn_ref[...], None, k_ref, v_ref)


def _ctx_kv_call(ckv, kr128, wk, wv, kn):
    n = ckv.shape[0]
    return pl.pallas_call(
        _ctx_kv_kernel,
        grid=(1,),
        in_specs=[_full(ckv.shape), _full(kr128.shape), _full(wk.shape), _full(wv.shape), _full((1, LANES))],
        out_specs=[_full((n, 1024)), _full((n, 512))],
        out_shape=[jax.ShapeDtypeStruct((n, 1024), BF16), jax.ShapeDtypeStruct((n, 512), BF16)],
        compiler_params=_cparams("arbitrary"),
        name="ctx_kv",
    )(ckv, kr128, wk, wv, kn)


def _attn_kernel(*refs, has_ctx):
    if has_ctx:
        q_ref, k_ref, v_ref, kc_ref, vc_ref, _, o_ref = refs
    else:
        q_ref, k_ref, v_ref, o_ref = refs
    nt = (((1,), (1,)), ((), ()))
    for p in range(MLA_HEADS // 2):
        psl = slice(p * LANES, (p + 1) * LANES)
        v2 = v_ref[:, psl]
        lane = lax.broadcasted_iota(jnp.int32, v2.shape, 1)
        vs = (jnp.where(lane < HALF, v2, jnp.zeros_like(v2)), jnp.where(lane >= HALF, v2, jnp.zeros_like(v2)))
        if has_ctx:
            vc2 = vc_ref[:, psl]
            lc = lax.broadcasted_iota(jnp.int32, vc2.shape, 1)
            vcs = (jnp.where(lc < HALF, vc2, jnp.zeros_like(vc2)), jnp.where(lc >= HALF, vc2, jnp.zeros_like(vc2)))
        acc = None
        for hh in range(2):
            sl = slice((2 * p + hh) * LANES, (2 * p + hh + 1) * LANES)
            qh = q_ref[:, sl]
            s = lax.dot_general(qh, k_ref[:, sl], nt, preferred_element_type=F32)
            m = jnp.max(s, axis=-1, keepdims=True)
            if has_ctx:
                sc = lax.dot_general(qh, kc_ref[:, sl], nt, preferred_element_type=F32)
                m = jnp.maximum(m, jnp.max(sc, axis=-1, keepdims=True))
            e = jnp.exp(s - m)
            l = jnp.sum(e, axis=-1, keepdims=True)
            o = jnp.dot(e.astype(BF16), vs[hh], preferred_element_type=F32)
            if has_ctx:
                ec = jnp.exp(sc - m)
                l = l + jnp.sum(ec, axis=-1, keepdims=True)
                o = o + jnp.dot(ec.astype(BF16), vcs[hh], preferred_element_type=F32)
            o = o / l
            acc = o if acc is None else acc + o
        o_ref[:, psl] = acc.astype(BF16)


def _attn_prompt_call(q, k, v):
    blk = lambda w_: pl.BlockSpec((SEQ, w_), lambda b: (b, 0))
    return pl.pallas_call(
        functools.partial(_attn_kernel, has_ctx=False),
        grid=(BATCH,),
        in_specs=[blk(1024), blk(1024), blk(512)],
        out_specs=blk(512),
        out_shape=jax.ShapeDtypeStruct((T_ALL, 512), BF16),
        compiler_params=_cparams("parallel"),
        name="attn_prompt",
    )(q, k, v)


def _attn_sample_call(q, k, v, kc, vc, out_buf):
    nq = DEC_SEQ // TQ
    qoff = NP_ROWS // TQ
    return pl.pallas_call(
        functools.partial(_attn_kernel, has_ctx=True),
        grid=(DEC_BATCH, nq),
        in_specs=[pl.BlockSpec((TQ, 1024), lambda b, i: (qoff + b * nq + i, 0)),
                  pl.BlockSpec((DEC_SEQ, 1024), lambda b, i: (1 + b, 0)),
                  pl.BlockSpec((DEC_SEQ, 512), lambda b, i: (1 + b, 0)),
                  pl.BlockSpec((PAST_LEN, 1024), lambda b, i: (b, 0)),
                  pl.BlockSpec((PAST_LEN, 512), lambda b, i: (b, 0)),
                  pl.BlockSpec(memory_space=pl.ANY)],
        out_specs=pl.BlockSpec((TQ, 512), lambda b, i: (qoff + b * nq + i, 0)),
        out_shape=jax.ShapeDtypeStruct((T_ALL, 512), BF16),
        input_output_aliases={5: 0},
        compiler_params=_cparams("parallel", "parallel"),
        name="attn_sample",
    )(q, k, v, kc, vc, out_buf)


def _segsum(y2, bd_ref):
    hi = y2.astype(BF16)
    lo = (y2 - hi.astype(F32)).astype(BF16)
    return (jnp.dot(hi, bd_ref[...], preferred_element_type=F32)
            + jnp.dot(lo, bd_ref[...], preferred_element_type=F32))


def _gdn_prep_kernel(x_ref, xp_ref, xn_ref, cw_ref, bd_ref, o_ref, *, tiles_per_seq_sample, prompt_tiles):
    i = pl.program_id(0)
    x = x_ref[...]
    tb = x.shape[0]
    in_sample = i >= prompt_tiles
    pos = (i - prompt_tiles) % tiles_per_seq_sample
    first = jnp.logical_or(jnp.logical_not(in_sample), pos == 0)
    last = jnp.logical_or(jnp.logical_not(in_sample), pos == tiles_per_seq_sample - 1)
    prev_row = xp_ref[7:8, :] * jnp.where(first, 0.0, 1.0)
    next_row = xn_ref[0:1, :] * jnp.where(last, 0.0, 1.0)
    ri = lax.broadcasted_iota(jnp.int32, x.shape, 0)
    xprev = jnp.where(ri == 0, prev_row, pltpu.roll(x, 1, 0))
    xnext = jnp.where(ri == tb - 1, next_row, pltpu.roll(x, tb - 1, 0))
    y = _silu(xprev * cw_ref[0:1, :] + x * cw_ref[1:2, :] + xnext * cw_ref[2:3, :])
    q = y[:, 0:512]
    k = y[:, 512:1024]
    o_ref[:, 0:512] = q * lax.rsqrt(_segsum(q * q, bd_ref) + EPS) * (GDN_DK ** -0.5)
    o_ref[:, 512:1024] = k * lax.rsqrt(_segsum(k * k, bd_ref) + EPS)
    o_ref[:, 1024:1536] = y[:, 1024:1536]


def _gdn_prep_call(gqkv, conv_w8, bd):
    tb = SCAN_ROWS
    n = T_ALL // tb
    r8 = tb // 8
    nb8 = T_ALL // 8
    return pl.pallas_call(
        functools.partial(_gdn_prep_kernel, tiles_per_seq_sample=DEC_SEQ // tb, prompt_tiles=NP_ROWS // tb),
        grid=(n,),
        in_specs=[pl.BlockSpec((tb, 1536), lambda i: (i, 0)),
                  pl.BlockSpec((8, 1536), lambda i: (jnp.maximum(i * r8 - 1, 0), 0)),
                  pl.BlockSpec((8, 1536), lambda i: (jnp.minimum((i + 1) * r8, nb8 - 1), 0)),
                  _full((8, 1536)), _full((512, 512))],
        out_specs=pl.BlockSpec((tb, 1536), lambda i: (i, 0)),
        out_shape=jax.ShapeDtypeStruct((T_ALL, 1536), F32),
        compiler_params=_cparams("parallel"),
        name="gdn_prep",
    )(gqkv, gqkv, gqkv, conv_w8, bd)


def _pair_masks(reverse):
    ri = lax.broadcasted_iota(jnp.int32, (CH, LANES), 0)
    li = lax.broadcasted_iota(jnp.int32, (CH, LANES), 1)
    jm = jnp.bitwise_and(li, HALF - 1)
    tri = (ri <= jm) if reverse else (ri >= jm)
    strict = (ri < jm) if reverse else (ri > jm)
    diag = ri == jm
    left = li < HALF
    r2 = lax.broadcasted_iota(jnp.int32, (2 * CH, LANES), 0)
    l2 = lax.broadcasted_iota(jnp.int32, (2 * CH, LANES), 1)
    bdm = (r2 < HALF) == (l2 < HALF)
    return tri, strict, diag, left, bdm


def _bd(x, bdm):
    return jnp.where(bdm, jnp.concatenate([x, x], axis=0), 0.0)


def _row_form(col, diag):
    return jnp.sum(jnp.where(diag, col, 0.0), axis=0, keepdims=True)


class _ScanLayout:
    def __init__(self, n_short, n_long, long_blks, reverse):
        self.n_short, self.n_long, self.long_blks, self.reverse = n_short, n_long, long_blks, reverse
        self.steps = n_short + n_long * long_blks
        self.n_seq = n_short + n_long

    def _pos(self, t):
        return (t - self.n_short) % self.long_blks

    def seq(self, t):
        return jnp.where(t < self.n_short, t, self.n_short + (t - self.n_short) // self.long_blks)

    def blk(self, t):
        j = self._pos(t)
        jj = (self.long_blks - 1 - j) if self.reverse else j
        return jnp.where(t < self.n_short, t, self.n_short + ((t - self.n_short) // self.long_blks) * self.long_blks + jj)

    def first(self, t):
        return jnp.logical_or(t < self.n_short, self._pos(t) == 0)

    def last(self, t):
        return jnp.logical_or(t < self.n_short, self._pos(t) == self.long_blks - 1)


def _gdn_scan_kernel(*refs, lay, n_chunks, has_add):
    if has_add:
        x_ref, g_ref, s0_ref, sel_ref, tri_ref, add_ref, o_ref, sfin_ref, s_scr = refs
    else:
        x_ref, g_ref, s0_ref, sel_ref, tri_ref, o_ref, sfin_ref, s_scr = refs
    reverse = lay.reverse
    step = pl.program_id(0)

    @pl.when(lay.first(step))
    def _():
        s_scr[...] = s0_ref[...]

    tri, strict, diag, _, bdm = _pair_masks(reverse)
    eye = jnp.where(diag, 1.0, 0.0)
    last = 0 if reverse else CH - 1
    ri = lax.broadcasted_iota(jnp.int32, (CH, LANES), 0)
    jm = jnp.bitwise_and(lax.broadcasted_iota(jnp.int32, (CH, LANES), 1), HALF - 1)
    same = [jnp.right_shift(ri, s) == jnp.right_shift(jm, s) for s in (3, 4, 5)]
    same8 = same[0]
    offs = (jnp.logical_and(same[1], jnp.logical_not(same[0])),
            jnp.logical_and(same[2], jnp.logical_not(same[1])),
            jnp.logical_not(same[2]))

    npair = GDN_HEADS // 2
    g = g_ref[...]
    gcs = _dot_x01(_dot_01x(tri_ref[...], g), sel_ref[:, 0:512])
    bex = _dot_x01(g, sel_ref[:, 512:1024])
    chains = [(c, p) for c in range(n_chunks) for p in range(npair)]

    st = {}
    for c, p in chains:
        rows = slice(c * CH, (c + 1) * CH)
        psl = slice(p * LANES, (p + 1) * LANES)
        q2 = x_ref[rows, p * LANES:(p + 1) * LANES]
        k2 = x_ref[rows, 512 + p * LANES:512 + (p + 1) * LANES]
        v2 = x_ref[rows, 1024 + p * LANES:1024 + (p + 1) * LANES]
        gcol = gcs[rows, psl]
        bcol = bex[rows, psl]
        grow = _row_form(gcol, diag)
        dec = jnp.exp(jnp.where(tri, gcol - grow, -jnp.inf))
        kb = k2 * bcol
        eg = jnp.exp(gcol)
        glrow = gcol[last:last + 1, :]
        ktbd = _bd(k2, bdm).T
        st[c, p] = dict(dec=dec, ktbd=ktbd, qd=q2 * eg, glrow=glrow,
                        lhs=jnp.concatenate([kb, q2], axis=0),
                        rhs=jnp.concatenate([_bd(v2 * bcol, bdm), _bd(kb * eg, bdm)], axis=1),
                        ktdec=ktbd * jnp.exp(glrow - grow))
    for key in chains:
        d = st[key]
        aq = _dot(d.pop("lhs"), d["ktbd"])
        dec = d.pop("dec")
        d["a"] = jnp.where(strict, aq[0:CH] * dec, 0.0)
        d["attn"] = aq[CH:2 * CH] * dec

    def bd2(parts):
        return tuple(jnp.where(bdm, jnp.concatenate([x, x], axis=0), jnp.zeros((), BF16)) for x in parts)

    for key in chains:
        d = st[key]
        b = -jnp.where(same8, d["a"], 0.0)
        d["t"] = eye + b
        d["ps"] = _split2(b)
        d["pb"] = bd2(d["ps"])
    for key in chains:
        d = st[key]
        d["ps"] = _split2(_dot_x3(d["ps"], d["pb"]))
        d["pb"] = bd2(d["ps"])
    for key in chains:
        d = st[key]
        d["t"] = d["t"] + _dot_x3(_split2(d["t"]), d["pb"])
        d["pb"] = bd2(_split2(_dot_x3(d.pop("ps"), d["pb"])))
    for key in chains:
        d = st[key]
        d["t"] = d["t"] + _dot_x3(_split2(d["t"]), d.pop("pb"))
    for off in offs:
        for key in chains:
            d = st[key]
            d["y"] = _dot(d["t"], _bd(jnp.where(off, d["a"], 0.0), bdm))
        for key in chains:
            d = st[key]
            d["t"] = d["t"] - _dot(d.pop("y"), _bd(d["t"], bdm))
    for key in chains:
        d = st[key]
        d.pop("a")
        d["uw"] = _dot(d.pop("t"), d.pop("rhs"))

    s = [s_scr[p] for p in range(npair)]
    for ci in range(n_chunks):
        c = (n_chunks - 1 - ci) if reverse else ci
        rows = slice(c * CH, (c + 1) * CH)
        wq = [_dot(jnp.concatenate([st[c, p]["uw"][:, LANES:2 * LANES], st[c, p]["qd"]], axis=0), s[p])
              for p in range(npair)]
        for p in range(npair):
            d = st[c, p]
            vnbd = _bd(d["uw"][:, 0:LANES] - wq[p][0:CH], bdm)
            psl = slice(p * LANES, (p + 1) * LANES)
            o = wq[p][CH:2 * CH] + _dot(d["attn"], vnbd)
            o_ref[rows, psl] = (o + add_ref[rows, psl]) if has_add else o
            s[p] = s[p] * jnp.exp(d["glrow"]) + _dot(d["ktdec"], vnbd)
    for p in range(npair):
        s_scr[p] = s[p]

    @pl.when(lay.last(step))
    def _():
        sfin_ref[...] = s_scr[...]


def _gdn_scan_call(xq, gates, s0, sel, tri, add, *, lay):
    rmap = lambda t: (lay.blk(t), 0)
    smap = lambda t: (lay.seq(t), 0, 0, 0)
    in_specs = [pl.BlockSpec((SCAN_ROWS, 1536), rmap), pl.BlockSpec((SCAN_ROWS, LANES), rmap),
                pl.BlockSpec((None, 4, LANES, LANES), smap), _full(sel.shape), _full((SCAN_ROWS, SCAN_ROWS))]
    args = [xq, gates, s0, sel, tri]
    if add is not None:
        in_specs.append(pl.BlockSpec((SCAN_ROWS, 512), rmap))
        args.append(add)
    return pl.pallas_call(
        functools.partial(_gdn_scan_kernel, lay=lay, n_chunks=SCAN_ROWS // CH, has_add=add is not None),
        grid=(lay.steps,),
        in_specs=in_specs,
        out_specs=[pl.BlockSpec((SCAN_ROWS, 512), rmap), pl.BlockSpec((None, 4, LANES, LANES), smap)],
        out_shape=[jax.ShapeDtypeStruct((lay.steps * SCAN_ROWS, 512), F32),
                   jax.ShapeDtypeStruct((lay.n_seq, 4, LANES, LANES), F32)],
        scratch_shapes=[pltpu.VMEM((4, LANES, LANES), F32)],
        compiler_params=_cparams("arbitrary"),
        name="gdn_scan_bwd" if lay.reverse else "gdn_scan_fwd",
    )(*args)


def _mlstm_scan_kernel(*refs, lay, n_chunks, has_add):
    if has_add:
        (q_ref, k_ref, v_ref, g_ref, c0_ref, m0_ref, sel_ref, tri_ref, add_ref,
         o_ref, cfin_ref, mfin_ref, c_scr, m_scr) = refs
    else:
        (q_ref, k_ref, v_ref, g_ref, c0_ref, m0_ref, sel_ref, tri_ref,
         o_ref, cfin_ref, mfin_ref, c_scr, m_scr) = refs
    reverse = lay.reverse
    step = pl.program_id(0)

    @pl.when(lay.first(step))
    def _():
        c_scr[...] = c0_ref[...]
        m_scr[...] = m0_ref[...]

    tri, _, diag, left, bdm = _pair_masks(reverse)
    left1 = left[0:1, :]
    last = 0 if reverse else CH - 1
    ninf = -jnp.inf
    ones = jnp.ones((CH, LANES), F32)
    rtop = lax.broadcasted_iota(jnp.int32, (2 * CH, 2 * LANES), 0) < HALF

    npair = MLSTM_HEADS // 2
    right = jnp.logical_not(left)
    g = g_ref[...]
    ex = _dot_x01(g, sel_ref[:, 0:1024])
    bfull = _dot_x01(_dot_01x(tri_ref[...], g), sel_ref[:, 1024:2048])
    chains = [(c, p) for c in range(n_chunks) for p in range(npair)]
    two = lambda a: jnp.concatenate([a, a], axis=1)

    st = {}
    for c, p in chains:
        rows = slice(c * CH, (c + 1) * CH)
        h0, h1 = 2 * p, 2 * p + 1
        psl = slice(p * LANES, (p + 1) * LANES)
        b0 = bfull[rows, h0 * LANES:(h0 + 1) * LANES]
        b1 = bfull[rows, h1 * LANES:(h1 + 1) * LANES]
        bcol = jnp.where(left, b0, b1)
        icol = jnp.where(left, ex[rows, h0 * LANES:(h0 + 1) * LANES], ex[rows, h1 * LANES:(h1 + 1) * LANES])
        brow = _row_form(bcol, diag)
        irow = _row_form(icol, diag)
        dlog = jnp.where(tri, bcol - brow + irow, ninf)
        dmax0 = jnp.max(jnp.where(left, dlog, ninf), axis=-1, keepdims=True)
        dmax1 = jnp.max(jnp.where(left, ninf, dlog), axis=-1, keepdims=True)
        bl0 = b0[last:last + 1, :]
        bl1 = b1[last:last + 1, :]
        glog = jnp.where(left1, bl0, bl1) - brow + irow
        gmax0 = jnp.max(jnp.where(left1, glog, ninf), axis=-1, keepdims=True)
        gmax1 = jnp.max(jnp.where(left1, ninf, glog), axis=-1, keepdims=True)
        q2 = q_ref[rows, psl]
        ktbd = _bd(k_ref[rows, psl], bdm).T
        vx = jnp.concatenate(
            [jnp.concatenate([v_ref[rows, h0 * LANES:(h0 + 1) * LANES], ones], axis=1),
             jnp.concatenate([v_ref[rows, h1 * LANES:(h1 + 1) * LANES], ones], axis=1)], axis=0)
        st[c, p] = dict(b=(b0, b1), dmax=(dmax0, dmax1), bl=(bl0, bl1), gmax=(gmax0, gmax1), q2=q2, ktbd=ktbd,
                        vxb=vx.astype(BF16), ew=jnp.exp(dlog - jnp.where(left, dmax0, dmax1)),
                        kws=ktbd * jnp.exp(glog - jnp.where(left1, gmax0, gmax1)))
    for key in chains:
        d = st[key]
        d["wts"] = d.pop("ew") * _dot(d["q2"], d.pop("ktbd"))
    for key in chains:
        d = st[key]
        wts = d.pop("wts")
        d["y"] = tuple(jnp.dot(jnp.where(keep, wts, 0.0).astype(BF16), d["vxb"], preferred_element_type=F32)
                       for keep in (left, right))
        d["upd"] = jnp.dot(d.pop("kws").astype(BF16), d.pop("vxb"), preferred_element_type=F32)

    cx = [c_scr[p] for p in range(npair)]
    m = [m_scr[h:h + 1, :] for h in range(MLSTM_HEADS)]
    for ci in range(n_chunks):
        c = (n_chunks - 1 - ci) if reverse else ci
        rows = slice(c * CH, (c + 1) * CH)
        for p in range(npair):
            d = st[c, p]
            cxb = cx[p].astype(BF16)
            dec, sc = [], []
            for hh, keep in enumerate((left, right)):
                hd = 2 * p + hh
                inter = d["b"][hh] + m[hd]
                mt = jnp.maximum(inter, d["dmax"][hh])
                z = jnp.dot(jnp.where(keep, d["q2"], 0.0).astype(BF16), cxb, preferred_element_type=F32)
                xo = d["y"][hh] * two(jnp.exp(d["dmax"][hh] - mt)) + two(jnp.exp(inter - mt)) * z
                den = jnp.maximum(jnp.abs(xo[:, LANES:2 * LANES]), jnp.exp(-mt))
                hsl = slice(hd * LANES, (hd + 1) * LANES)
                o_ref[rows, hsl] = (xo[:, 0:LANES] / den + add_ref[rows, hsl]) if has_add else xo[:, 0:LANES] / den
                mn = jnp.maximum(d["bl"][hh] + m[hd], d["gmax"][hh])
                dec.append(two(jnp.exp(d["bl"][hh] + m[hd] - mn)))
                sc.append(two(jnp.exp(d["gmax"][hh] - mn)))
                m[hd] = mn
            cx[p] = jnp.where(rtop, dec[0], dec[1]) * cx[p] + jnp.where(rtop, sc[0], sc[1]) * d["upd"]
    for p in range(npair):
        c_scr[p] = cx[p]
    for h in range(MLSTM_HEADS):
        m_scr[h:h + 1, :] = m[h]

    @pl.when(lay.last(step))
    def _():
        cfin_ref[...] = c_scr[...]
        mfin_ref[...] = m_scr[...]


def _mlstm_scan_call(q, k, v, gates, c0, m0, sel, tri, add, *, lay):
    rmap = lambda t: (lay.blk(t), 0)
    st = lambda *shape: pl.BlockSpec((None,) + shape, lambda t: (lay.seq(t),) + (0,) * len(shape))
    in_specs = [pl.BlockSpec((SCAN_ROWS, 512), rmap), pl.BlockSpec((SCAN_ROWS, 512), rmap),
                pl.BlockSpec((SCAN_ROWS, 1024), rmap), pl.BlockSpec((SCAN_ROWS, LANES), rmap),
                st(4, LANES, 2 * LANES), st(8, LANES), _full(sel.shape), _full((SCAN_ROWS, SCAN_ROWS))]
    args = [q, k, v, gates, c0, m0, sel, tri]
    if add is not None:
        in_specs.append(pl.BlockSpec((SCAN_ROWS, 1024), rmap))
        args.append(add)
    return pl.pallas_call(
        functools.partial(_mlstm_scan_kernel, lay=lay, n_chunks=SCAN_ROWS // CH, has_add=add is not None),
        grid=(lay.steps,),
        in_specs=in_specs,
        out_specs=[pl.BlockSpec((SCAN_ROWS, 1024), rmap), st(4, LANES, 2 * LANES), st(8, LANES)],
        out_shape=[jax.ShapeDtypeStruct((lay.steps * SCAN_ROWS, 1024), F32),
                   jax.ShapeDtypeStruct((lay.n_seq, 4, LANES, 2 * LANES), F32),
                   jax.ShapeDtypeStruct((lay.n_seq, 8, LANES), F32)],
        scratch_shapes=[pltpu.VMEM((4, LANES, 2 * LANES), F32), pltpu.VMEM((8, LANES), F32)],
        compiler_params=_cparams("arbitrary"),
        name="mlstm_scan_bwd" if lay.reverse else "mlstm_scan_fwd",
    )(*args)


def _odd_in_kernel(x_ref, m_ref, nw_ref, w_ref, gb_ref, q_ref, k_ref, v_ref, og_ref, gate_ref):
    h = _norm_mod(x_ref[...], nw_ref[...], m_ref[1:2, :], m_ref[0:1, :])
    proj = jnp.dot(h.astype(BF16), w_ref[...], preferred_element_type=F32)
    q_ref[...] = proj[:, 0:512]
    k_ref[...] = proj[:, 512:1024] * (MLSTM_DK ** -0.5)
    v_ref[...] = proj[:, 1024:2048]
    og_ref[...] = proj[:, 2048:3072]
    g = proj[:, 3072:3200] + gb_ref[...]
    lane = lax.broadcasted_iota(jnp.int32, g.shape, 1)
    gate_ref[...] = jnp.where(lane < 16, g, -_softplus(-g))


def _odd_in_call(x, mod3, nw, w, gbias):
    tm = TM_IN
    nt = GROUP // tm
    row = lambda w_: pl.BlockSpec((tm, w_), lambda i: (i, 0))
    outs = [512, 512, 1024, 1024, 128]
    return pl.pallas_call(
        _odd_in_kernel,
        grid=(T_ALL // tm,),
        in_specs=[row(D_MODEL), pl.BlockSpec((None, 8, D_MODEL), lambda i: (i // nt, 0, 0)),
                  _full((1, D_MODEL)), _full(w.shape), _full((1, LANES))],
        out_specs=[row(w_) for w_ in outs],
        out_shape=[jax.ShapeDtypeStruct((T_ALL, w_), F32) for w_ in outs],
        compiler_params=_cparams("parallel"),
        name="odd_in",
    )(x, mod3, nw, w, gbias)


NWC = D_FF // TF
WO_ROWS = D_MODEL // 8


def _ffn_kernel(*refs, even):
    if even:
        (x_ref, m_ref, nw_ref, attn_ref, os_ref, gz_ref, onw_ref, bd_ref, wo_ref, wg_ref, wu_ref, wout_ref,
         o_ref, wo_scr, wgu_scr, wout_scr) = refs
    else:
        (x_ref, m_ref, nw_ref, hs_ref, og_ref, onw_ref, wo_ref, wg_ref, wu_ref, wout_ref,
         o_ref, wo_scr, wgu_scr, wout_scr) = refs
    t = pl.program_id(0)

    @pl.when(t < NWC)
    def _():
        wgu_scr[t, :, 0:TF] = wg_ref[...].astype(BF16)
        wgu_scr[t, :, TF:2 * TF] = wu_ref[...].astype(BF16)
        wout_scr[pl.ds(pl.multiple_of(t * TF, TF), TF), :] = wout_ref[...].astype(BF16)

    @pl.when(t < D_MODEL // WO_ROWS)
    def _():
        wo_scr[pl.ds(pl.multiple_of(t * WO_ROWS, WO_ROWS), WO_ROWS), :] = wo_ref[...].astype(BF16)

    @pl.when(t >= NWC)
    def _():
        if even:
            d = os_ref[...]
            ms = _segsum(d * d, bd_ref) * (1.0 / GDN_DV)
            dn = d * lax.rsqrt(ms + EPS) * onw_ref[...] * _silu(gz_ref[...])
            mixed = (jnp.dot(attn_ref[...], wo_scr[0:512, :], preferred_element_type=F32)
                     + jnp.dot(dn.astype(BF16), wo_scr[512:1024, :], preferred_element_type=F32))
        else:
            d = hs_ref[...]
            parts = []
            for h in range(MLSTM_HEADS):
                dh = d[:, h * LANES:(h + 1) * LANES]
                parts.append(dh * lax.rsqrt(jnp.mean(dh * dh, axis=-1, keepdims=True) + EPS))
            mem = jnp.concatenate(parts, axis=1) * onw_ref[...] * jax.nn.sigmoid(og_ref[...])
            mixed = jnp.dot(mem.astype(BF16), wo_scr[...], preferred_element_type=F32)
        x1 = x_ref[...] + m_ref[2:3, :] * mixed
        hn = _norm_mod(x1, nw_ref[...], m_ref[4:5, :], m_ref[3:4, :]).astype(BF16)
        acts = []
        for c in range(NWC):
            gu = jnp.dot(hn, wgu_scr[c], preferred_element_type=F32)
            acts.append((_silu(gu[:, 0:TF]) * gu[:, TF:2 * TF]).astype(BF16))
        y = jnp.dot(jnp.concatenate(acts, axis=1), wout_scr[...], preferred_element_type=F32)
        o_ref[...] = x1 + m_ref[5:6, :] * y


def _ffn_call(even, l, x, mod3, nw, mix_inputs, onw, bd, wo_all, w_in_all, w_out_all):
    tm = TM_FFN
    nt = GROUP // tm
    tile = lambda t: jnp.maximum(t - NWC, 0)
    wstep = lambda t: jnp.minimum(t, NWC - 1)
    row = lambda w_: pl.BlockSpec((tm, w_), lambda t: (tile(t), 0))
    const = lambda shape: pl.BlockSpec(shape, lambda t: (0,) * len(shape))
    in_specs = [row(D_MODEL), pl.BlockSpec((None, 8, D_MODEL), lambda t: (tile(t) // nt, 0, 0)), const((1, D_MODEL))]
    in_specs += [row(a.shape[1]) for a in mix_inputs]
    args = [x, mod3, nw] + list(mix_inputs) + [onw]
    in_specs.append(const(onw.shape))
    if even:
        in_specs.append(const(bd.shape))
        args.append(bd)
    in_specs += [pl.BlockSpec((None, WO_ROWS, D_MODEL), lambda t: (l // 2, jnp.minimum(t, D_MODEL // WO_ROWS - 1), 0)),
                 pl.BlockSpec((None, D_MODEL, TF), lambda t: (l, 0, wstep(t))),
                 pl.BlockSpec((None, D_MODEL, TF), lambda t: (l, 0, NWC + wstep(t))),
                 pl.BlockSpec((None, TF, D_MODEL), lambda t: (l, wstep(t), 0))]
    args += [wo_all, w_in_all, w_in_all, w_out_all]
    return pl.pallas_call(
        functools.partial(_ffn_kernel, even=even),
        grid=(NWC + T_ALL // tm,),
        in_specs=in_specs,
        out_specs=row(D_MODEL),
        out_shape=jax.ShapeDtypeStruct((T_ALL, D_MODEL), F32),
        scratch_shapes=[pltpu.VMEM((D_MODEL, D_MODEL), BF16), pltpu.VMEM((NWC, D_MODEL, 2 * TF), BF16),
                        pltpu.VMEM((D_FF, D_MODEL), BF16)],
        compiler_params=_cparams("arbitrary"),
        name="ffn_even" if even else "ffn_odd",
    )(*args)


def _rope_tables():
    rows = DEC_SEQ // GRID_W
    row = jnp.repeat(jnp.arange(rows, dtype=F32), GRID_W)
    col = jnp.tile(jnp.arange(GRID_W, dtype=F32), rows)
    per_axis = MLA_ROPE // 2
    inv = 1.0 / (ROPE_BASE ** (jnp.arange(0, per_axis, 2, dtype=F32) / per_axis))
    ar = row[:, None] * inv
    ac = col[:, None] * inv
    ang = jnp.concatenate([ar, ar, ac, ac], axis=-1)
    cos, sin = jnp.cos(ang), jnp.sin(ang)
    first = np.tile(np.concatenate([np.ones(8), np.zeros(8)]), 2).astype(np.float32)
    pad = lambda a, fill: jnp.concatenate(
        [jnp.full((DEC_SEQ, MLA_NOPE), fill, F32), a, jnp.full((DEC_SEQ, LANES - MLA_QK), fill, F32)], axis=1)
    cos_t = pad(cos, 1.0)
    sa_t = pad(-sin * first, 0.0)
    sb_t = pad(sin * (1.0 - first), 0.0)
    ident = lambda fill: jnp.full((GROUP, LANES), fill, F32)
    return (jnp.concatenate([ident(1.0), cos_t]), jnp.concatenate([ident(0.0), sa_t]),
            jnp.concatenate([ident(0.0), sb_t]))


def _pad_cols(a, left, total):
    return jnp.pad(a, ((0, 0), (left, total - left - a.shape[1])))


def _lane_row(vec, total=LANES):
    return jnp.pad(vec.reshape(1, -1).astype(F32), ((0, 0), (0, total - vec.size)))


def _gdn_sel(direction):
    j = np.arange(512)
    head = 2 * (j // LANES) + (j % LANES >= HALF)
    sel = np.zeros((LANES, 1024), np.float32)
    sel[direction * 8 + head, j] = 1.0
    sel[16 + direction * 8 + head, 512 + j] = 1.0
    return jnp.asarray(sel, dtype=BF16)


def _mlstm_sel(direction):
    j = np.arange(1024)
    sel = np.zeros((LANES, 2048), np.float32)
    sel[direction * 8 + j // LANES, j] = 1.0
    sel[16 + direction * 8 + j // LANES, 1024 + j] = 1.0
    return jnp.asarray(sel, dtype=BF16)


def _tri_blocks(reverse):
    t = np.tril(np.ones((CH, CH), np.float32))
    return jnp.asarray(np.kron(np.eye(SCAN_ROWS // CH, dtype=np.float32), t.T if reverse else t), dtype=BF16)


def _block_diag_pairs(s):
    lead = s.shape[:-3]
    s = s.reshape(lead + (4, 2, HALF, HALF))
    z = jnp.zeros_like(s[..., 0, :, :])
    top = jnp.concatenate([s[..., 0, :, :], z], axis=-1)
    bot = jnp.concatenate([z, s[..., 1, :, :]], axis=-1)
    return jnp.concatenate([top, bot], axis=-2)


def _unblock_diag_pairs(s):
    a = s[..., :HALF, :HALF]
    b = s[..., HALF:, HALF:]
    return jnp.stack([a, b], axis=-3).reshape(s.shape[:-3] + (8, HALF, HALF))


def kernel(x_prompt, x_sample, c, c_ctx, cache_mla_ckv, cache_mla_krope, state_gdn, state_mlstm_C, state_mlstm_n,
           state_mlstm_m, norm_mix, norm_ffn, w_ada, b_ada, w_ffn_in, w_ffn_out, w_even_in, w_even_out,
           mla_q_a_norm, mla_kv_a_norm, w_mla_uq, w_mla_ukv, mla_q_norm, mla_k_norm, gdn_conv, gdn_a_log,
           gdn_dt_bias, gdn_out_norm, w_odd_in, w_odd_out, mlstm_gate_bias, mlstm_out_norm):
    x = jnp.concatenate([x_prompt.reshape(NP_ROWS, D_MODEL), x_sample.reshape(NS_ROWS, D_MODEL)], axis=0)

    cond8 = jnp.concatenate([c_ctx[None, :], c, jnp.zeros((8 - 1 - DEC_BATCH, D_MODEL), F32)], axis=0)
    mods = _ada_call(cond8, w_ada, b_ada)
    mods = mods[:, :3].reshape(DEPTH, 3, 6, D_MODEL)
    mods = jnp.pad(mods, ((0, 0), (0, 0), (0, 2), (0, 0)))

    cos_t, sa_t, sb_t = _rope_tables()
    bd512 = jnp.asarray(np.kron(np.eye(8, dtype=np.float32), np.ones((HALF, HALF), np.float32))).astype(BF16)
    trib_f, trib_b = _tri_blocks(False), _tri_blocks(True)
    assert SEQ == SCAN_ROWS
    lay_f = _ScanLayout(BATCH, DEC_BATCH, DEC_SEQ // SCAN_ROWS, False)
    lay_b = _ScanLayout(BATCH, DEC_BATCH, DEC_SEQ // SCAN_ROWS, True)

    ckv_new, krope_new, gdn_new, mc_new, mn_new, mm_new = [], [], [], [], [], []
    for l in range(DEPTH):
        j = l // 2
        mod3 = mods[l]
        nw1 = norm_mix[l].reshape(1, D_MODEL)
        nw2 = norm_ffn[l].reshape(1, D_MODEL)
        if l % 2 == 0:
            wi = w_even_in[j].astype(BF16)
            w_arr = jnp.concatenate([wi[:, 0:640], _pad_cols(wi[:, 640:672], MLA_NOPE, LANES), wi[:, 672:2720],
                                     _pad_cols(wi[:, 2720:2752], 0, LANES)], axis=1)
            wuq = jnp.pad(w_mla_uq[j].astype(BF16).reshape(Q_LORA, MLA_HEADS, MLA_QK),
                          ((0, 0), (0, 0), (0, LANES - MLA_QK))).reshape(Q_LORA, MLA_HEADS * LANES)
            wukv = w_mla_ukv[j].astype(BF16).reshape(KV_LORA, MLA_HEADS, MLA_NOPE + MLA_V)
            wk = jnp.pad(wukv[:, :, :MLA_NOPE], ((0, 0), (0, 0), (0, LANES - MLA_NOPE))).reshape(KV_LORA, -1)
            wv = wukv[:, :, MLA_NOPE:].reshape(KV_LORA, MLA_HEADS * MLA_V)
            qn = _lane_row(mla_q_norm[j])
            kn = _lane_row(mla_k_norm[j])
            gp = jnp.concatenate([_lane_row(gdn_a_log[j]), _lane_row(gdn_dt_bias[j]), jnp.zeros((6, LANES), F32)])
            q, k, v, ckvn, kr, gqkv, gz, gates = _even_in_call(
                x, mod3, nw1, w_arr, mla_q_a_norm[j].reshape(1, -1), mla_kv_a_norm[j].reshape(1, -1),
                wuq, wk, wv, qn, kn, cos_t, sa_t, sb_t, gp)
            ckv_new.append(ckvn[:NP_ROWS].reshape(BATCH, SEQ, KV_LORA))
            krope_new.append(kr[:NP_ROWS, MLA_NOPE:MLA_QK].reshape(BATCH, SEQ, MLA_ROPE))
            kc, vc = _ctx_kv_call(cache_mla_ckv[:, j].reshape(DEC_BATCH * PAST_LEN, KV_LORA),
                                  _pad_cols(cache_mla_krope[:, j].reshape(DEC_BATCH * PAST_LEN, MLA_ROPE),
                                            MLA_NOPE, LANES), wk, wv, kn)
            attn = _attn_sample_call(q, k, v, kc, vc, _attn_prompt_call(q, k, v))
            conv8 = jnp.pad(gdn_conv[j], ((0, 5), (0, 0)))
            xq = _gdn_prep_call(gqkv, conv8, bd512)
            zeros_p = jnp.zeros((BATCH, 4, LANES, LANES), F32)
            osum, s_dirs = None, []
            for d, (lay, tri_d) in enumerate(((lay_f, trib_f), (lay_b, trib_b))):
                s0 = jnp.concatenate([zeros_p, _block_diag_pairs(state_gdn[:, j, d])], axis=0)
                osum, sfin = _gdn_scan_call(xq, gates, s0, _gdn_sel(d), tri_d, osum, lay=lay)
                s_dirs.append(_unblock_diag_pairs(sfin[:BATCH]))
            gdn_new.append(jnp.stack(s_dirs, axis=1))
            onw = jnp.tile(gdn_out_norm[j], GDN_HEADS).reshape(1, -1)
            x = _ffn_call(True, l, x, mod3, nw2, [attn, osum, gz], onw, bd512, w_even_out, w_ffn_in, w_ffn_out)
        else:
            wi = w_odd_in[j].astype(BF16)
            w_arr = jnp.concatenate([wi[:, :3072], _pad_cols(wi[:, 3072:3104], 0, LANES)], axis=1)
            q, k, v, og, gates = _odd_in_call(x, mod3, nw1, w_arr, _lane_row(mlstm_gate_bias[j]))
            hsum, c_dirs, m_dirs = None, [], []
            for d, (lay, tri_d) in enumerate(((lay_f, trib_f), (lay_b, trib_b))):
                c0 = state_mlstm_C[:, j, d]
                n0 = state_mlstm_n[:, j, d]
                cx0 = jnp.concatenate([c0, jnp.broadcast_to(n0[..., None], c0.shape)], axis=-1)
                cx0 = jnp.concatenate([jnp.zeros((BATCH, 4, LANES, 2 * LANES), F32),
                                       cx0.reshape(DEC_BATCH, 4, LANES, 2 * LANES)], axis=0)
                m0 = jnp.concatenate([jnp.zeros((BATCH, 8, LANES), F32),
                                      jnp.broadcast_to(state_mlstm_m[:, j, d][..., None], (DEC_BATCH, 8, LANES))], axis=0)
                hsum, cfin, mfin = _mlstm_scan_call(q, k, v, gates, cx0, m0, _mlstm_sel(d), tri_d, hsum, lay=lay)
                c_dirs.append(cfin[:BATCH].reshape(BATCH, 8, HALF, 2 * LANES))
                m_dirs.append(mfin[:BATCH, :, 0])
            cst = jnp.stack(c_dirs, axis=1)
            mc_new.append(cst[..., :LANES])
            mn_new.append(cst[..., LANES])
            mm_new.append(jnp.stack(m_dirs, axis=1))
            x = _ffn_call(False, l, x, mod3, nw2, [hsum, og], mlstm_out_norm[j].reshape(1, -1), None,
                          w_odd_out, w_ffn_in, w_ffn_out)

    return (x[:NP_ROWS].reshape(BATCH, SEQ, D_MODEL), x[NP_ROWS:].reshape(DEC_BATCH, DEC_SEQ, D_MODEL),
            jnp.stack(ckv_new, axis=1), jnp.stack(krope_new, axis=1), jnp.stack(gdn_new, axis=1),
            jnp.stack(mc_new, axis=1), jnp.stack(mn_new, axis=1), jnp.stack(mm_new, axis=1))
```

```python
import functools

import jax
import jax.numpy as jnp
import numpy as np
from jax import lax
from jax.experimental import pallas as pl
from jax.experimental.pallas import tpu as pltpu

F32 = jnp.float32
BF16 = jnp.bfloat16
HIGHEST = lax.Precision.HIGHEST

D_MODEL = 1024
BATCH = 16
SEQ = 256
DEPTH = 4
DEC_BATCH = 2
DEC_SEQ = 4096
PAST_LEN = 256
GRID_W = 64
EPS = 1e-6
ROPE_BASE = 10000.0
MLA_HEADS = 8
MLA_NOPE = 64
MLA_ROPE = 32
MLA_QK = MLA_NOPE + MLA_ROPE
MLA_V = 64
Q_LORA = 384
KV_LORA = 256
GDN_HEADS = 8
GDN_DK = 64
GDN_DV = 64
MLSTM_HEADS = 8
MLSTM_DK = 64
MLSTM_DV = 128
D_FF = 2816

LANES = 128
HALF = 64
CH = 64
NP_ROWS = BATCH * SEQ
NS_ROWS = DEC_BATCH * DEC_SEQ
T_ALL = NP_ROWS + NS_ROWS
GROUP = 4096
assert NP_ROWS == GROUP and DEC_SEQ == GROUP

TM_IN = 256
TM_FFN = 512
TF = 256
TQ = 256
SCAN_ROWS = 256
VMEM_LIMIT = 56 * 1024 * 1024

EVEN_W = 384 + 256 + 128 + 1536 + 512 + 128
ODD_W = 512 + 512 + 1024 + 1024 + 128


def _cparams(*sem):
    return pltpu.CompilerParams(dimension_semantics=sem, vmem_limit_bytes=VMEM_LIMIT)


def _silu(x):
    return x * jax.nn.sigmoid(x)


def _softplus(x):
    return jnp.maximum(x, 0.0) + jnp.log(1.0 + jnp.exp(-jnp.abs(x)))


def _dot(a, b):
    return jnp.dot(a.astype(BF16), b.astype(BF16), preferred_element_type=F32)


def _dot_hi(a, b):
    return jnp.dot(a, b, precision=HIGHEST, preferred_element_type=F32)


def _split3(x):
    hi = x.astype(BF16)
    r = x - hi.astype(F32)
    mid = r.astype(BF16)
    lo = (r - mid.astype(F32)).astype(BF16)
    return hi, mid, lo


def _split2(x):
    hi = x.astype(BF16)
    return hi, (x - hi.astype(F32)).astype(BF16)


def _dot_x3(a, b):
    f = lambda x, y: jnp.dot(x, y, preferred_element_type=F32)
    return f(a[0], b[0]) + f(a[0], b[1]) + f(a[1], b[0])


def _dot_x01(x, m01):
    return sum(jnp.dot(part, m01, preferred_element_type=F32) for part in _split3(x))


def _dot_01x(m01, x):
    return sum(jnp.dot(m01, part, preferred_element_type=F32) for part in _split3(x))


def _norm_mod(x, nw, sc, sh):
    ms = jnp.mean(x * x, axis=-1, keepdims=True)
    return (x * lax.rsqrt(ms + EPS) * nw) * (1.0 + sc) + sh


def _ada_kernel(c_ref, w_ref, b_ref, o_ref):
    c = c_ref[...]
    o_ref[...] = _dot(_silu(c), w_ref[...]) + b_ref[...]


def _ada_call(cond8, w_ada, b_ada):
    tn = 1536
    n = w_ada.shape[-1]
    return pl.pallas_call(
        _ada_kernel,
        grid=(DEPTH, n // tn),
        in_specs=[pl.BlockSpec((8, D_MODEL), lambda l, j: (0, 0)),
                  pl.BlockSpec((None, D_MODEL, tn), lambda l, j: (l, 0, j)),
                  pl.BlockSpec((None, 1, tn), lambda l, j: (l, 0, j))],
        out_specs=pl.BlockSpec((None, 8, tn), lambda l, j: (l, 0, j)),
        out_shape=jax.ShapeDtypeStruct((DEPTH, 8, n), F32),
        compiler_params=_cparams("parallel", "parallel"),
        name="ada",
    )(cond8, w_ada, b_ada.reshape(DEPTH, 1, n))


def _rope(x, cos, sa, sb):
    return x * cos + pltpu.roll(x, LANES - 8, 1) * sa + pltpu.roll(x, 8, 1) * sb


def _head_norm_rope(x, w, rope, scale):
    ms = jnp.sum(x * x, axis=-1, keepdims=True) * (1.0 / MLA_QK)
    y = x * lax.rsqrt(ms + EPS) * w
    if rope is not None:
        y = _rope(y, *rope)
    if scale != 1.0:
        y = y * scale
    return y


def _mla_kv(ckvn, kr, wk_ref, wv_ref, kn, rope, k_ref, v_ref):
    cb = ckvn.astype(BF16)
    kraw = jnp.dot(cb, wk_ref[...], preferred_element_type=F32)
    for h in range(MLA_HEADS):
        sl = slice(h * LANES, (h + 1) * LANES)
        k_ref[:, sl] = _head_norm_rope(kraw[:, sl] + kr, kn, rope, 1.0).astype(BF16)
    v_ref[...] = jnp.dot(cb, wv_ref[...], preferred_element_type=F32).astype(BF16)


def _even_in_kernel(x_ref, m_ref, nw_ref, w_ref, qan_ref, kvan_ref, wuq_ref, wk_ref, wv_ref, qn_ref, kn_ref,
                    cos_ref, sa_ref, sb_ref, gp_ref,
                    q_ref, k_ref, v_ref, ckv_ref, kr_ref, gqkv_ref, gz_ref, gate_ref):
    x = x_ref[...]
    h = _norm_mod(x, nw_ref[...], m_ref[1:2, :], m_ref[0:1, :])
    proj = jnp.dot(h.astype(BF16), w_ref[...], preferred_element_type=F32)
    rope = (cos_ref[...], sa_ref[...], sb_ref[...])
    cq = proj[:, 0:384]
    cqn = cq * lax.rsqrt(jnp.mean(cq * cq, axis=-1, keepdims=True) + EPS) * qan_ref[...]
    qraw = jnp.dot(cqn.astype(BF16), wuq_ref[...], preferred_element_type=F32)
    qn = qn_ref[...]
    for hd in range(MLA_HEADS):
        sl = slice(hd * LANES, (hd + 1) * LANES)
        q_ref[:, sl] = _head_norm_rope(qraw[:, sl], qn, rope, MLA_QK ** -0.5).astype(BF16)
    ckv = proj[:, 384:640]
    ckvn = ckv * lax.rsqrt(jnp.mean(ckv * ckv, axis=-1, keepdims=True) + EPS) * kvan_ref[...]
    ckv_ref[...] = ckvn
    kr = proj[:, 640:768]
    kr_ref[...] = kr
    _mla_kv(ckvn, kr, wk_ref, wv_ref, kn_ref[...], rope, k_ref, v_ref)
    gqkv_ref[...] = proj[:, 768:2304]
    gz_ref[...] = proj[:, 2304:2816]
    g = proj[:, 2816:2944]
    lane = lax.broadcasted_iota(jnp.int32, g.shape, 1)
    log_decay = -jnp.exp(gp_ref[0:1, :]) * _softplus(g + gp_ref[1:2, :])
    gate_ref[...] = jnp.where(lane < 16, log_decay, jax.nn.sigmoid(g))


def _full(shape):
    nd = len(shape)
    return pl.BlockSpec(shape, lambda *_: (0,) * nd)


def _even_in_call(x, mod3, nw, w, qan, kvan, wuq, wk, wv, qn, kn, cos, sa, sb, gp):
    tm = TM_IN
    nt = GROUP // tm
    row = lambda w_: pl.BlockSpec((tm, w_), lambda i: (i, 0))
    rope_spec = pl.BlockSpec((tm, LANES), lambda i: (jnp.where(i < nt, i, nt + i % nt), 0))
    outs = [(1024, BF16), (1024, BF16), (512, BF16), (256, F32), (128, F32), (1536, F32), (512, F32), (128, F32)]
    return pl.pallas_call(
        _even_in_kernel,
        grid=(T_ALL // tm,),
        in_specs=[row(D_MODEL),
                  pl.BlockSpec((None, 8, D_MODEL), lambda i: (i // nt, 0, 0)),
                  _full((1, D_MODEL)), _full(w.shape), _full((1, 384)), _full((1, 256)),
                  _full(wuq.shape), _full(wk.shape), _full(wv.shape), _full((1, LANES)), _full((1, LANES)),
                  rope_spec, rope_spec, rope_spec, _full((8, LANES))],
        out_specs=[row(w_) for w_, _ in outs],
        out_shape=[jax.ShapeDtypeStruct((T_ALL, w_), dt) for w_, dt in outs],
        compiler_params=_cparams("parallel"),
        name="even_in",
    )(x, mod3, nw, w, qan, kvan, wuq, wk, wv, qn, kn, cos, sa, sb, gp)


def _ctx_kv_kernel(ckv_ref, kr_ref, wk_ref, wv_ref, kn_ref, k_ref, v_ref):
    _mla_kv(ckv_ref[...], kr_ref[...], wk_ref, wv_ref, kn_ref[...], None, k_ref, v_ref)


def _ctx_kv_call(ckv, kr128, wk, wv, kn):
    n = ckv.shape[0]
    return pl.pallas_call(
        _ctx_kv_kernel,
        grid=(1,),
        in_specs=[_full(ckv.shape), _full(kr128.shape), _full(wk.shape), _full(wv.shape), _full((1, LANES))],
        out_specs=[_full((n, 1024)), _full((n, 512))],
        out_shape=[jax.ShapeDtypeStruct((n, 1024), BF16), jax.ShapeDtypeStruct((n, 512), BF16)],
        compiler_params=_cparams("arbitrary"),
        name="ctx_kv",
    )(ckv, kr128, wk, wv, kn)


def _attn_kernel(*refs, has_ctx):
    if has_ctx:
        q_ref, k_ref, v_ref, kc_ref, vc_ref, _, o_ref = refs
    else:
        q_ref, k_ref, v_ref, o_ref = refs
    nt = (((1,), (1,)), ((), ()))
    for p in range(MLA_HEADS // 2):
        psl = slice(p * LANES, (p + 1) * LANES)
        v2 = v_ref[:, psl]
        lane = lax.broadcasted_iota(jnp.int32, v2.shape, 1)
        vs = (jnp.where(lane < HALF, v2, jnp.zeros_like(v2)), jnp.where(lane >= HALF, v2, jnp.zeros_like(v2)))
        if has_ctx:
            vc2 = vc_ref[:, psl]
            lc = lax.broadcasted_iota(jnp.int32, vc2.shape, 1)
            vcs = (jnp.where(lc < HALF, vc2, jnp.zeros_like(vc2)), jnp.where(lc >= HALF, vc2, jnp.zeros_like(vc2)))
        acc = None
        for hh in range(2):
            sl = slice((2 * p + hh) * LANES, (2 * p + hh + 1) * LANES)
            qh = q_ref[:, sl]
            s = lax.dot_general(qh, k_ref[:, sl], nt, preferred_element_type=F32)
            m = jnp.max(s, axis=-1, keepdims=True)
            if has_ctx:
                sc = lax.dot_general(qh, kc_ref[:, sl], nt, preferred_element_type=F32)
                m = jnp.maximum(m, jnp.max(sc, axis=-1, keepdims=True))
            e = jnp.exp((s - m).astype(BF16))
            l = jnp.sum(e, axis=-1, keepdims=True, dtype=F32)
            o = jnp.dot(e, vs[hh], preferred_element_type=F32)
            if has_ctx:
                ec = jnp.exp((sc - m).astype(BF16))
                l = l + jnp.sum(ec, axis=-1, keepdims=True, dtype=F32)
                o = o + jnp.dot(ec, vcs[hh], preferred_element_type=F32)
            o = o / l
            acc = o if acc is None else acc + o
        o_ref[:, psl] = acc.astype(BF16)


def _attn_prompt_call(q, k, v):
    blk = lambda w_: pl.BlockSpec((SEQ, w_), lambda b: (b, 0))
    return pl.pallas_call(
        functools.partial(_attn_kernel, has_ctx=False),
        grid=(BATCH,),
        in_specs=[blk(1024), blk(1024), blk(512)],
        out_specs=blk(512),
        out_shape=jax.ShapeDtypeStruct((T_ALL, 512), BF16),
        compiler_params=_cparams("parallel"),
        name="attn_prompt",
    )(q, k, v)


def _attn_sample_call(q, k, v, kc, vc, out_buf):
    nq = DEC_SEQ // TQ
    qoff = NP_ROWS // TQ
    return pl.pallas_call(
        functools.partial(_attn_kernel, has_ctx=True),
        grid=(DEC_BATCH, nq),
        in_specs=[pl.BlockSpec((TQ, 1024), lambda b, i: (qoff + b * nq + i, 0)),
                  pl.BlockSpec((DEC_SEQ, 1024), lambda b, i: (1 + b, 0)),
                  pl.BlockSpec((DEC_SEQ, 512), lambda b, i: (1 + b, 0)),
                  pl.BlockSpec((PAST_LEN, 1024), lambda b, i: (b, 0)),
                  pl.BlockSpec((PAST_LEN, 512), lambda b, i: (b, 0)),
                  pl.BlockSpec(memory_space=pl.ANY)],
        out_specs=pl.BlockSpec((TQ, 512), lambda b, i: (qoff + b * nq + i, 0)),
        out_shape=jax.ShapeDtypeStruct((T_ALL, 512), BF16),
        input_output_aliases={5: 0},
        compiler_params=_cparams("parallel", "parallel"),
        name="attn_sample",
    )(q, k, v, kc, vc, out_buf)


def _segsum(y2, bd_ref):
    hi = y2.astype(BF16)
    lo = (y2 - hi.astype(F32)).astype(BF16)
    return (jnp.dot(hi, bd_ref[...], preferred_element_type=F32)
            + jnp.dot(lo, bd_ref[...], preferred_element_type=F32))


def _gdn_prep_kernel(x_ref, xp_ref, xn_ref, cw_ref, bd_ref, o_ref, *, tiles_per_seq_sample, prompt_tiles):
    i = pl.program_id(0)
    x = x_ref[...]
    tb = x.shape[0]
    in_sample = i >= prompt_tiles
    pos = (i - prompt_tiles) % tiles_per_seq_sample
    first = jnp.logical_or(jnp.logical_not(in_sample), pos == 0)
    last = jnp.logical_or(jnp.logical_not(in_sample), pos == tiles_per_seq_sample - 1)
    prev_row = xp_ref[7:8, :] * jnp.where(first, 0.0, 1.0)
    next_row = xn_ref[0:1, :] * jnp.where(last, 0.0, 1.0)
    ri = lax.broadcasted_iota(jnp.int32, x.shape, 0)
    xprev = jnp.where(ri == 0, prev_row, pltpu.roll(x, 1, 0))
    xnext = jnp.where(ri == tb - 1, next_row, pltpu.roll(x, tb - 1, 0))
    y = _silu(xprev * cw_ref[0:1, :] + x * cw_ref[1:2, :] + xnext * cw_ref[2:3, :])
    q = y[:, 0:512]
    k = y[:, 512:1024]
    o_ref[:, 0:512] = q * lax.rsqrt(_segsum(q * q, bd_ref) + EPS) * (GDN_DK ** -0.5)
    o_ref[:, 512:1024] = k * lax.rsqrt(_segsum(k * k, bd_ref) + EPS)
    o_ref[:, 1024:1536] = y[:, 1024:1536]


def _gdn_prep_call(gqkv, conv_w8, bd):
    tb = SCAN_ROWS
    n = T_ALL // tb
    r8 = tb // 8
    nb8 = T_ALL // 8
    return pl.pallas_call(
        functools.partial(_gdn_prep_kernel, tiles_per_seq_sample=DEC_SEQ // tb, prompt_tiles=NP_ROWS // tb),
        grid=(n,),
        in_specs=[pl.BlockSpec((tb, 1536), lambda i: (i, 0)),
                  pl.BlockSpec((8, 1536), lambda i: (jnp.maximum(i * r8 - 1, 0), 0)),
                  pl.BlockSpec((8, 1536), lambda i: (jnp.minimum((i + 1) * r8, nb8 - 1), 0)),
                  _full((8, 1536)), _full((512, 512))],
        out_specs=pl.BlockSpec((tb, 1536), lambda i: (i, 0)),
        out_shape=jax.ShapeDtypeStruct((T_ALL, 1536), F32),
        compiler_params=_cparams("parallel"),
        name="gdn_prep",
    )(gqkv, gqkv, gqkv, conv_w8, bd)


def _pair_masks(reverse):
    ri = lax.broadcasted_iota(jnp.int32, (CH, LANES), 0)
    li = lax.broadcasted_iota(jnp.int32, (CH, LANES), 1)
    jm = jnp.bitwise_and(li, HALF - 1)
    tri = (ri <= jm) if reverse else (ri >= jm)
    strict = (ri < jm) if reverse else (ri > jm)
    diag = ri == jm
    left = li < HALF
    r2 = lax.broadcasted_iota(jnp.int32, (2 * CH, LANES), 0)
    l2 = lax.broadcasted_iota(jnp.int32, (2 * CH, LANES), 1)
    bdm = (r2 < HALF) == (l2 < HALF)
    return tri, strict, diag, left, bdm


def _bd(x, bdm):
    return jnp.where(bdm, jnp.concatenate([x, x], axis=0), 0.0)


def _row_form(col, diag):
    return jnp.sum(jnp.where(diag, col, 0.0), axis=0, keepdims=True)


class _ScanLayout:
    def __init__(self, n_short, n_long, long_blks, reverse):
        self.n_short, self.n_long, self.long_blks, self.reverse = n_short, n_long, long_blks, reverse
        self.steps = n_short + n_long * long_blks
        self.n_seq = n_short + n_long

    def _pos(self, t):
        return (t - self.n_short) % self.long_blks

    def seq(self, t):
        return jnp.where(t < self.n_short, t, self.n_short + (t - self.n_short) // self.long_blks)

    def blk(self, t):
        j = self._pos(t)
        jj = (self.long_blks - 1 - j) if self.reverse else j
        return jnp.where(t < self.n_short, t, self.n_short + ((t - self.n_short) // self.long_blks) * self.long_blks + jj)

    def first(self, t):
        return jnp.logical_or(t < self.n_short, self._pos(t) == 0)

    def last(self, t):
        return jnp.logical_or(t < self.n_short, self._pos(t) == self.long_blks - 1)


def _gdn_scan_kernel(*refs, lay, n_chunks, has_add):
    if has_add:
        x_ref, g_ref, s0_ref, sel_ref, tri_ref, add_ref, o_ref, sfin_ref, s_scr = refs
    else:
        x_ref, g_ref, s0_ref, sel_ref, tri_ref, o_ref, sfin_ref, s_scr = refs
    reverse = lay.reverse
    step = pl.program_id(0)

    @pl.when(lay.first(step))
    def _():
        s_scr[...] = s0_ref[...]

    tri, strict, diag, _, bdm = _pair_masks(reverse)
    eye = jnp.where(diag, 1.0, 0.0)
    last = 0 if reverse else CH - 1
    ri = lax.broadcasted_iota(jnp.int32, (CH, LANES), 0)
    jm = jnp.bitwise_and(lax.broadcasted_iota(jnp.int32, (CH, LANES), 1), HALF - 1)
    same = [jnp.right_shift(ri, s) == jnp.right_shift(jm, s) for s in (3, 4, 5)]
    same8 = same[0]
    offs = (jnp.logical_and(same[1], jnp.logical_not(same[0])),
            jnp.logical_and(same[2], jnp.logical_not(same[1])),
            jnp.logical_not(same[2]))

    npair = GDN_HEADS // 2
    g = g_ref[...]
    gcs = _dot_x01(_dot_01x(tri_ref[...], g), sel_ref[:, 0:512])
    bex = _dot_x01(g, sel_ref[:, 512:1024])
    chains = [(c, p) for c in range(n_chunks) for p in range(npair)]

    st = {}
    for c, p in chains:
        rows = slice(c * CH, (c + 1) * CH)
        psl = slice(p * LANES, (p + 1) * LANES)
        q2 = x_ref[rows, p * LANES:(p + 1) * LANES]
        k2 = x_ref[rows, 512 + p * LANES:512 + (p + 1) * LANES]
        v2 = x_ref[rows, 1024 + p * LANES:1024 + (p + 1) * LANES]
        gcol = gcs[rows, psl]
        bcol = bex[rows, psl]
        grow = _row_form(gcol, diag)
        dec = jnp.exp(jnp.where(tri, gcol - grow, -jnp.inf))
        kb = k2 * bcol
        eg = jnp.exp(gcol)
        glrow = gcol[last:last + 1, :]
        ktbd = _bd(k2, bdm).T
        st[c, p] = dict(dec=dec, ktbd=ktbd, qd=q2 * eg, glrow=glrow,
                        lhs=jnp.concatenate([kb, q2], axis=0),
                        rhs=jnp.concatenate([_bd(v2 * bcol, bdm), _bd(kb * eg, bdm)], axis=1),
                        ktdec=ktbd * jnp.exp(glrow - grow))
    for key in chains:
        d = st[key]
        aq = _dot(d.pop("lhs"), d["ktbd"])
        dec = d.pop("dec")
        d["a"] = jnp.where(strict, aq[0:CH] * dec, 0.0)
        d["attn"] = aq[CH:2 * CH] * dec

    def bd2(parts):
        return tuple(jnp.where(bdm, jnp.concatenate([x, x], axis=0), jnp.zeros((), BF16)) for x in parts)

    for key in chains:
        d = st[key]
        b = -jnp.where(same8, d["a"], 0.0)
        d["t"] = eye + b
        d["ps"] = _split2(b)
        d["pb"] = bd2(d["ps"])
    for key in chains:
        d = st[key]
        d["ps"] = _split2(_dot_x3(d["ps"], d["pb"]))
        d["pb"] = bd2(d["ps"])
    for key in chains:
        d = st[key]
        d["t"] = d["t"] + _dot_x3(_split2(d["t"]), d["pb"])
        d["pb"] = bd2(_split2(_dot_x3(d.pop("ps"), d["pb"])))
    for key in chains:
        d = st[key]
        d["t"] = d["t"] + _dot_x3(_split2(d["t"]), d.pop("pb"))
    for off in offs:
        for key in chains:
            d = st[key]
            d["y"] = _dot(d["t"], _bd(jnp.where(off, d["a"], 0.0), bdm))
        for key in chains:
            d = st[key]
            d["t"] = d["t"] - _dot(d.pop("y"), _bd(d["t"], bdm))
    for key in chains:
        d = st[key]
        d.pop("a")
        d["uw"] = _dot(d.pop("t"), d.pop("rhs"))

    s = [s_scr[p] for p in range(npair)]
    for ci in range(n_chunks):
        c = (n_chunks - 1 - ci) if reverse else ci
        rows = slice(c * CH, (c + 1) * CH)
        wq = [_dot(jnp.concatenate([st[c, p]["uw"][:, LANES:2 * LANES], st[c, p]["qd"]], axis=0), s[p])
              for p in range(npair)]
        for p in range(npair):
            d = st[c, p]
            vnbd = _bd(d["uw"][:, 0:LANES] - wq[p][0:CH], bdm)
            psl = slice(p * LANES, (p + 1) * LANES)
            o = wq[p][CH:2 * CH] + _dot(d["attn"], vnbd)
            o_ref[rows, psl] = (o + add_ref[rows, psl]) if has_add else o
            s[p] = s[p] * jnp.exp(d["glrow"]) + _dot(d["ktdec"], vnbd)
    for p in range(npair):
        s_scr[p] = s[p]

    @pl.when(lay.last(step))
    def _():
        sfin_ref[...] = s_scr[...]


def _gdn_scan_call(xq, gates, s0, sel, tri, add, *, lay):
    rmap = lambda t: (lay.blk(t), 0)
    smap = lambda t: (lay.seq(t), 0, 0, 0)
    in_specs = [pl.BlockSpec((SCAN_ROWS, 1536), rmap), pl.BlockSpec((SCAN_ROWS, LANES), rmap),
                pl.BlockSpec((None, 4, LANES, LANES), smap), _full(sel.shape), _full((SCAN_ROWS, SCAN_ROWS))]
    args = [xq, gates, s0, sel, tri]
    if add is not None:
        in_specs.append(pl.BlockSpec((SCAN_ROWS, 512), rmap))
        args.append(add)
    return pl.pallas_call(
        functools.partial(_gdn_scan_kernel, lay=lay, n_chunks=SCAN_ROWS // CH, has_add=add is not None),
        grid=(lay.steps,),
        in_specs=in_specs,
        out_specs=[pl.BlockSpec((SCAN_ROWS, 512), rmap), pl.BlockSpec((None, 4, LANES, LANES), smap)],
        out_shape=[jax.ShapeDtypeStruct((lay.steps * SCAN_ROWS, 512), F32),
                   jax.ShapeDtypeStruct((lay.n_seq, 4, LANES, LANES), F32)],
        scratch_shapes=[pltpu.VMEM((4, LANES, LANES), F32)],
        compiler_params=_cparams("arbitrary"),
        name="gdn_scan_bwd" if lay.reverse else "gdn_scan_fwd",
    )(*args)


def _mlstm_scan_kernel(*refs, lay, n_chunks, has_add):
    if has_add:
        (q_ref, k_ref, v_ref, g_ref, c0_ref, m0_ref, sel_ref, tri_ref, add_ref,
         o_ref, cfin_ref, mfin_ref, c_scr, m_scr) = refs
    else:
        (q_ref, k_ref, v_ref, g_ref, c0_ref, m0_ref, sel_ref, tri_ref,
         o_ref, cfin_ref, mfin_ref, c_scr, m_scr) = refs
    reverse = lay.reverse
    step = pl.program_id(0)

    @pl.when(lay.first(step))
    def _():
        c_scr[...] = c0_ref[...]
        m_scr[...] = m0_ref[...]

    tri, _, diag, left, bdm = _pair_masks(reverse)
    left1 = left[0:1, :]
    last = 0 if reverse else CH - 1
    ninf = -jnp.inf
    ones = jnp.ones((CH, LANES), F32)
    rtop = lax.broadcasted_iota(jnp.int32, (2 * CH, 2 * LANES), 0) < HALF

    npair = MLSTM_HEADS // 2
    right = jnp.logical_not(left)
    g = g_ref[...]
    ex = _dot_x01(g, sel_ref[:, 0:1024])
    bfull = _dot_x01(_dot_01x(tri_ref[...], g), sel_ref[:, 1024:2048])
    chains = [(c, p) for c in range(n_chunks) for p in range(npair)]
    two = lambda a: jnp.concatenate([a, a], axis=1)

    st = {}
    for c, p in chains:
        rows = slice(c * CH, (c + 1) * CH)
        h0, h1 = 2 * p, 2 * p + 1
        psl = slice(p * LANES, (p + 1) * LANES)
        b0 = bfull[rows, h0 * LANES:(h0 + 1) * LANES]
        b1 = bfull[rows, h1 * LANES:(h1 + 1) * LANES]
        bcol = jnp.where(left, b0, b1)
        icol = jnp.where(left, ex[rows, h0 * LANES:(h0 + 1) * LANES], ex[rows, h1 * LANES:(h1 + 1) * LANES])
        brow = _row_form(bcol, diag)
        irow = _row_form(icol, diag)
        dlog = jnp.where(tri, bcol - brow + irow, ninf)
        dmax0 = jnp.max(jnp.where(left, dlog, ninf), axis=-1, keepdims=True)
        dmax1 = jnp.max(jnp.where(left, ninf, dlog), axis=-1, keepdims=True)
        bl0 = b0[last:last + 1, :]
        bl1 = b1[last:last + 1, :]
        glog = jnp.where(left1, bl0, bl1) - brow + irow
        gmax0 = jnp.max(jnp.where(left1, glog, ninf), axis=-1, keepdims=True)
        gmax1 = jnp.max(jnp.where(left1, ninf, glog), axis=-1, keepdims=True)
        q2 = q_ref[rows, psl]
        ktbd = _bd(k_ref[rows, psl], bdm).T
        vx = jnp.concatenate(
            [jnp.concatenate([v_ref[rows, h0 * LANES:(h0 + 1) * LANES], ones], axis=1),
             jnp.concatenate([v_ref[rows, h1 * LANES:(h1 + 1) * LANES], ones], axis=1)], axis=0)
        st[c, p] = dict(b=(b0, b1), dmax=(dmax0, dmax1), bl=(bl0, bl1), gmax=(gmax0, gmax1), q2=q2, ktbd=ktbd,
                        vxb=vx.astype(BF16), ew=jnp.exp(dlog - jnp.where(left, dmax0, dmax1)),
                        kws=ktbd * jnp.exp(glog - jnp.where(left1, gmax0, gmax1)))
    for key in chains:
        d = st[key]
        d["wts"] = d.pop("ew") * _dot(d["q2"], d.pop("ktbd"))
    for key in chains:
        d = st[key]
        wts = d.pop("wts")
        d["y"] = tuple(jnp.dot(jnp.where(keep, wts, 0.0).astype(BF16), d["vxb"], preferred_element_type=F32)
                       for keep in (left, right))
        d["upd"] = jnp.dot(d.pop("kws").astype(BF16), d.pop("vxb"), preferred_element_type=F32)

    cx = [c_scr[p] for p in range(npair)]
    m = [m_scr[h:h + 1, :] for h in range(MLSTM_HEADS)]
    for ci in range(n_chunks):
        c = (n_chunks - 1 - ci) if reverse else ci
        rows = slice(c * CH, (c + 1) * CH)
        for p in range(npair):
            d = st[c, p]
            cxb = cx[p].astype(BF16)
            dec, sc = [], []
            for hh, keep in enumerate((left, right)):
                hd = 2 * p + hh
                inter = d["b"][hh] + m[hd]
                mt = jnp.maximum(inter, d["dmax"][hh])
                z = jnp.dot(jnp.where(keep, d["q2"], 0.0).astype(BF16), cxb, preferred_element_type=F32)
                xo = d["y"][hh] * two(jnp.exp(d["dmax"][hh] - mt)) + two(jnp.exp(inter - mt)) * z
                den = jnp.maximum(jnp.abs(xo[:, LANES:2 * LANES]), jnp.exp(-mt))
                hsl = slice(hd * LANES, (hd + 1) * LANES)
                o_ref[rows, hsl] = (xo[:, 0:LANES] / den + add_ref[rows, hsl]) if has_add else xo[:, 0:LANES] / den
                mn = jnp.maximum(d["bl"][hh] + m[hd], d["gmax"][hh])
                dec.append(two(jnp.exp(d["bl"][hh] + m[hd] - mn)))
                sc.append(two(jnp.exp(d["gmax"][hh] - mn)))
                m[hd] = mn
            cx[p] = jnp.where(rtop, dec[0], dec[1]) * cx[p] + jnp.where(rtop, sc[0], sc[1]) * d["upd"]
    for p in range(npair):
        c_scr[p] = cx[p]
    for h in range(MLSTM_HEADS):
        m_scr[h:h + 1, :] = m[h]

    @pl.when(lay.last(step))
    def _():
        cfin_ref[...] = c_scr[...]
        mfin_ref[...] = m_scr[...]


def _mlstm_scan_call(q, k, v, gates, c0, m0, sel, tri, add, *, lay):
    rmap = lambda t: (lay.blk(t), 0)
    st = lambda *shape: pl.BlockSpec((None,) + shape, lambda t: (lay.seq(t),) + (0,) * len(shape))
    in_specs = [pl.BlockSpec((SCAN_ROWS, 512), rmap), pl.BlockSpec((SCAN_ROWS, 512), rmap),
                pl.BlockSpec((SCAN_ROWS, 1024), rmap), pl.BlockSpec((SCAN_ROWS, LANES), rmap),
                st(4, LANES, 2 * LANES), st(8, LANES), _full(sel.shape), _full((SCAN_ROWS, SCAN_ROWS))]
    args = [q, k, v, gates, c0, m0, sel, tri]
    if add is not None:
        in_specs.append(pl.BlockSpec((SCAN_ROWS, 1024), rmap))
        args.append(add)
    return pl.pallas_call(
        functools.partial(_mlstm_scan_kernel, lay=lay, n_chunks=SCAN_ROWS // CH, has_add=add is not None),
        grid=(lay.steps,),
        in_specs=in_specs,
        out_specs=[pl.BlockSpec((SCAN_ROWS, 1024), rmap), st(4, LANES, 2 * LANES), st(8, LANES)],
        out_shape=[jax.ShapeDtypeStruct((lay.steps * SCAN_ROWS, 1024), F32),
                   jax.ShapeDtypeStruct((lay.n_seq, 4, LANES, 2 * LANES), F32),
                   jax.ShapeDtypeStruct((lay.n_seq, 8, LANES), F32)],
        scratch_shapes=[pltpu.VMEM((4, LANES, 2 * LANES), F32), pltpu.VMEM((8, LANES), F32)],
        compiler_params=_cparams("arbitrary"),
        name="mlstm_scan_bwd" if lay.reverse else "mlstm_scan_fwd",
    )(*args)


def _odd_in_kernel(x_ref, m_ref, nw_ref, w_ref, gb_ref, q_ref, k_ref, v_ref, og_ref, gate_ref):
    h = _norm_mod(x_ref[...], nw_ref[...], m_ref[1:2, :], m_ref[0:1, :])
    proj = jnp.dot(h.astype(BF16), w_ref[...], preferred_element_type=F32)
    q_ref[...] = proj[:, 0:512]
    k_ref[...] = proj[:, 512:1024] * (MLSTM_DK ** -0.5)
    v_ref[...] = proj[:, 1024:2048]
    og_ref[...] = proj[:, 2048:3072]
    g = proj[:, 3072:3200] + gb_ref[...]
    lane = lax.broadcasted_iota(jnp.int32, g.shape, 1)
    gate_ref[...] = jnp.where(lane < 16, g, -_softplus(-g))


def _odd_in_call(x, mod3, nw, w, gbias):
    tm = TM_IN
    nt = GROUP // tm
    row = lambda w_: pl.BlockSpec((tm, w_), lambda i: (i, 0))
    outs = [512, 512, 1024, 1024, 128]
    return pl.pallas_call(
        _odd_in_kernel,
        grid=(T_ALL // tm,),
        in_specs=[row(D_MODEL), pl.BlockSpec((None, 8, D_MODEL), lambda i: (i // nt, 0, 0)),
                  _full((1, D_MODEL)), _full(w.shape), _full((1, LANES))],
        out_specs=[row(w_) for w_ in outs],
        out_shape=[jax.ShapeDtypeStruct((T_ALL, w_), F32) for w_ in outs],
        compiler_params=_cparams("parallel"),
        name="odd_in",
    )(x, mod3, nw, w, gbias)


NWC = D_FF // TF
WO_ROWS = D_MODEL // 8


def _ffn_kernel(*refs, even):
    if even:
        (x_ref, m_ref, nw_ref, attn_ref, os_ref, gz_ref, onw_ref, bd_ref, wo_ref, wg_ref, wu_ref, wout_ref,
         o_ref, wo_scr, wgu_scr, wout_scr) = refs
    else:
        (x_ref, m_ref, nw_ref, hs_ref, og_ref, onw_ref, wo_ref, wg_ref, wu_ref, wout_ref,
         o_ref, wo_scr, wgu_scr, wout_scr) = refs
    t = pl.program_id(0)

    @pl.when(t < NWC)
    def _():
        wgu_scr[t, :, 0:TF] = wg_ref[...].astype(BF16)
        wgu_scr[t, :, TF:2 * TF] = wu_ref[...].astype(BF16)
        wout_scr[pl.ds(pl.multiple_of(t * TF, TF), TF), :] = wout_ref[...].astype(BF16)

    @pl.when(t < D_MODEL // WO_ROWS)
    def _():
        wo_scr[pl.ds(pl.multiple_of(t * WO_ROWS, WO_ROWS), WO_ROWS), :] = wo_ref[...].astype(BF16)

    @pl.when(t >= NWC)
    def _():
        if even:
            d = os_ref[...]
            ms = _segsum(d * d, bd_ref) * (1.0 / GDN_DV)
            dn = d * lax.rsqrt(ms + EPS) * onw_ref[...] * _silu(gz_ref[...])
            mixed = (jnp.dot(attn_ref[...], wo_scr[0:512, :], preferred_element_type=F32)
                     + jnp.dot(dn.astype(BF16), wo_scr[512:1024, :], preferred_element_type=F32))
        else:
            d = hs_ref[...]
            parts = []
            for h in range(MLSTM_HEADS):
                dh = d[:, h * LANES:(h + 1) * LANES]
                parts.append(dh * lax.rsqrt(jnp.mean(dh * dh, axis=-1, keepdims=True) + EPS))
            mem = jnp.concatenate(parts, axis=1) * onw_ref[...] * jax.nn.sigmoid(og_ref[...])
            mixed = jnp.dot(mem.astype(BF16), wo_scr[...], preferred_element_type=F32)
        x1 = x_ref[...] + m_ref[2:3, :] * mixed
        hn = _norm_mod(x1, nw_ref[...], m_ref[4:5, :], m_ref[3:4, :]).astype(BF16)
        acts = []
        for c in range(NWC):
            gu = jnp.dot(hn, wgu_scr[c], preferred_element_type=F32)
            acts.append((_silu(gu[:, 0:TF]) * gu[:, TF:2 * TF]).astype(BF16))
        y = jnp.dot(jnp.concatenate(acts, axis=1), wout_scr[...], preferred_element_type=F32)
        o_ref[...] = x1 + m_ref[5:6, :] * y


def _ffn_call(even, l, x, mod3, nw, mix_inputs, onw, bd, wo_all, w_in_all, w_out_all):
    tm = TM_FFN
    nt = GROUP // tm
    tile = lambda t: jnp.maximum(t - NWC, 0)
    wstep = lambda t: jnp.minimum(t, NWC - 1)
    row = lambda w_: pl.BlockSpec((tm, w_), lambda t: (tile(t), 0))
    const = lambda shape: pl.BlockSpec(shape, lambda t: (0,) * len(shape))
    in_specs = [row(D_MODEL), pl.BlockSpec((None, 8, D_MODEL), lambda t: (tile(t) // nt, 0, 0)), const((1, D_MODEL))]
    in_specs += [row(a.shape[1]) for a in mix_inputs]
    args = [x, mod3, nw] + list(mix_inputs) + [onw]
    in_specs.append(const(onw.shape))
    if even:
        in_specs.append(const(bd.shape))
        args.append(bd)
    in_specs += [pl.BlockSpec((None, WO_ROWS, D_MODEL), lambda t: (l // 2, jnp.minimum(t, D_MODEL // WO_ROWS - 1), 0)),
                 pl.BlockSpec((None, D_MODEL, TF), lambda t: (l, 0, wstep(t))),
                 pl.BlockSpec((None, D_MODEL, TF), lambda t: (l, 0, NWC + wstep(t))),
                 pl.BlockSpec((None, TF, D_MODEL), lambda t: (l, wstep(t), 0))]
    args += [wo_all, w_in_all, w_in_all, w_out_all]
    return pl.pallas_call(
        functools.partial(_ffn_kernel, even=even),
        grid=(NWC + T_ALL // tm,),
        in_specs=in_specs,
        out_specs=row(D_MODEL),
        out_shape=jax.ShapeDtypeStruct((T_ALL, D_MODEL), F32),
        scratch_shapes=[pltpu.VMEM((D_MODEL, D_MODEL), BF16), pltpu.VMEM((NWC, D_MODEL, 2 * TF), BF16),
                        pltpu.VMEM((D_FF, D_MODEL), BF16)],
        compiler_params=_cparams("arbitrary"),
        name="ffn_even" if even else "ffn_odd",
    )(*args)


def _rope_tables():
    rows = DEC_SEQ // GRID_W
    row = jnp.repeat(jnp.arange(rows, dtype=F32), GRID_W)
    col = jnp.tile(jnp.arange(GRID_W, dtype=F32), rows)
    per_axis = MLA_ROPE // 2
    inv = 1.0 / (ROPE_BASE ** (jnp.arange(0, per_axis, 2, dtype=F32) / per_axis))
    ar = row[:, None] * inv
    ac = col[:, None] * inv
    ang = jnp.concatenate([ar, ar, ac, ac], axis=-1)
    cos, sin = jnp.cos(ang), jnp.sin(ang)
    first = np.tile(np.concatenate([np.ones(8), np.zeros(8)]), 2).astype(np.float32)
    pad = lambda a, fill: jnp.concatenate(
        [jnp.full((DEC_SEQ, MLA_NOPE), fill, F32), a, jnp.full((DEC_SEQ, LANES - MLA_QK), fill, F32)], axis=1)
    cos_t = pad(cos, 1.0)
    sa_t = pad(-sin * first, 0.0)
    sb_t = pad(sin * (1.0 - first), 0.0)
    ident = lambda fill: jnp.full((GROUP, LANES), fill, F32)
    return (jnp.concatenate([ident(1.0), cos_t]), jnp.concatenate([ident(0.0), sa_t]),
            jnp.concatenate([ident(0.0), sb_t]))


def _pad_cols(a, left, total):
    return jnp.pad(a, ((0, 0), (left, total - left - a.shape[1])))


def _lane_row(vec, total=LANES):
    return jnp.pad(vec.reshape(1, -1).astype(F32), ((0, 0), (0, total - vec.size)))


def _gdn_sel(direction):
    j = np.arange(512)
    head = 2 * (j // LANES) + (j % LANES >= HALF)
    sel = np.zeros((LANES, 1024), np.float32)
    sel[direction * 8 + head, j] = 1.0
    sel[16 + direction * 8 + head, 512 + j] = 1.0
    return jnp.asarray(sel, dtype=BF16)


def _mlstm_sel(direction):
    j = np.arange(1024)
    sel = np.zeros((LANES, 2048), np.float32)
    sel[direction * 8 + j // LANES, j] = 1.0
    sel[16 + direction * 8 + j // LANES, 1024 + j] = 1.0
    return jnp.asarray(sel, dtype=BF16)


def _tri_blocks(reverse):
    t = np.tril(np.ones((CH, CH), np.float32))
    return jnp.asarray(np.kron(np.eye(SCAN_ROWS // CH, dtype=np.float32), t.T if reverse else t), dtype=BF16)


def _block_diag_pairs(s):
    lead = s.shape[:-3]
    s = s.reshape(lead + (4, 2, HALF, HALF))
    z = jnp.zeros_like(s[..., 0, :, :])
    top = jnp.concatenate([s[..., 0, :, :], z], axis=-1)
    bot = jnp.concatenate([z, s[..., 1, :, :]], axis=-1)
    return jnp.concatenate([top, bot], axis=-2)


def _unblock_diag_pairs(s):
    a = s[..., :HALF, :HALF]
    b = s[..., HALF:, HALF:]
    return jnp.stack([a, b], axis=-3).reshape(s.shape[:-3] + (8, HALF, HALF))


def kernel(x_prompt, x_sample, c, c_ctx, cache_mla_ckv, cache_mla_krope, state_gdn, state_mlstm_C, state_mlstm_n,
           state_mlstm_m, norm_mix, norm_ffn, w_ada, b_ada, w_ffn_in, w_ffn_out, w_even_in, w_even_out,
           mla_q_a_norm, mla_kv_a_norm, w_mla_uq, w_mla_ukv, mla_q_norm, mla_k_norm, gdn_conv, gdn_a_log,
           gdn_dt_bias, gdn_out_norm, w_odd_in, w_odd_out, mlstm_gate_bias, mlstm_out_norm):
    x = jnp.concatenate([x_prompt.reshape(NP_ROWS, D_MODEL), x_sample.reshape(NS_ROWS, D_MODEL)], axis=0)

    cond8 = jnp.concatenate([c_ctx[None, :], c, jnp.zeros((8 - 1 - DEC_BATCH, D_MODEL), F32)], axis=0)
    mods = _ada_call(cond8, w_ada, b_ada)
    mods = mods[:, :3].reshape(DEPTH, 3, 6, D_MODEL)
    mods = jnp.pad(mods, ((0, 0), (0, 0), (0, 2), (0, 0)))

    cos_t, sa_t, sb_t = _rope_tables()
    bd512 = jnp.asarray(np.kron(np.eye(8, dtype=np.float32), np.ones((HALF, HALF), np.float32))).astype(BF16)
    trib_f, trib_b = _tri_blocks(False), _tri_blocks(True)
    assert SEQ == SCAN_ROWS
    lay_f = _ScanLayout(BATCH, DEC_BATCH, DEC_SEQ // SCAN_ROWS, False)
    lay_b = _ScanLayout(BATCH, DEC_BATCH, DEC_SEQ // SCAN_ROWS, True)

    ckv_new, krope_new, gdn_new, mc_new, mn_new, mm_new = [], [], [], [], [], []
    for l in range(DEPTH):
        j = l // 2
        mod3 = mods[l]
        nw1 = norm_mix[l].reshape(1, D_MODEL)
        nw2 = norm_ffn[l].reshape(1, D_MODEL)
        if l % 2 == 0:
            wi = w_even_in[j].astype(BF16)
            w_arr = jnp.concatenate([wi[:, 0:640], _pad_cols(wi[:, 640:672], MLA_NOPE, LANES), wi[:, 672:2720],
                                     _pad_cols(wi[:, 2720:2752], 0, LANES)], axis=1)
            wuq = jnp.pad(w_mla_uq[j].astype(BF16).reshape(Q_LORA, MLA_HEADS, MLA_QK),
                          ((0, 0), (0, 0), (0, LANES - MLA_QK))).reshape(Q_LORA, MLA_HEADS * LANES)
            wukv = w_mla_ukv[j].astype(BF16).reshape(KV_LORA, MLA_HEADS, MLA_NOPE + MLA_V)
            wk = jnp.pad(wukv[:, :, :MLA_NOPE], ((0, 0), (0, 0), (0, LANES - MLA_NOPE))).reshape(KV_LORA, -1)
            wv = wukv[:, :, MLA_NOPE:].reshape(KV_LORA, MLA_HEADS * MLA_V)
            qn = _lane_row(mla_q_norm[j])
            kn = _lane_row(mla_k_norm[j])
            gp = jnp.concatenate([_lane_row(gdn_a_log[j]), _lane_row(gdn_dt_bias[j]), jnp.zeros((6, LANES), F32)])
            q, k, v, ckvn, kr, gqkv, gz, gates = _even_in_call(
                x, mod3, nw1, w_arr, mla_q_a_norm[j].reshape(1, -1), mla_kv_a_norm[j].reshape(1, -1),
                wuq, wk, wv, qn, kn, cos_t, sa_t, sb_t, gp)
            ckv_new.append(ckvn[:NP_ROWS].reshape(BATCH, SEQ, KV_LORA))
            krope_new.append(kr[:NP_ROWS, MLA_NOPE:MLA_QK].reshape(BATCH, SEQ, MLA_ROPE))
            kc, vc = _ctx_kv_call(cache_mla_ckv[:, j].reshape(DEC_BATCH * PAST_LEN, KV_LORA),
                                  _pad_cols(cache_mla_krope[:, j].reshape(DEC_BATCH * PAST_LEN, MLA_ROPE),
                                            MLA_NOPE, LANES), wk, wv, kn)
            attn = _attn_sample_call(q, k, v, kc, vc, _attn_prompt_call(q, k, v))
            conv8 = jnp.pad(gdn_conv[j], ((0, 5), (0, 0)))
            xq = _gdn_prep_call(gqkv, conv8, bd512)
            zeros_p = jnp.zeros((BATCH, 4, LANES, LANES), F32)
            osum, s_dirs = None, []
            for d, (lay, tri_d) in enumerate(((lay_f, trib_f), (lay_b, trib_b))):
                s0 = jnp.concatenate([zeros_p, _block_diag_pairs(state_gdn[:, j, d])], axis=0)
                osum, sfin = _gdn_scan_call(xq, gates, s0, _gdn_sel(d), tri_d, osum, lay=lay)
                s_dirs.append(_unblock_diag_pairs(sfin[:BATCH]))
            gdn_new.append(jnp.stack(s_dirs, axis=1))
            onw = jnp.tile(gdn_out_norm[j], GDN_HEADS).reshape(1, -1)
            x = _ffn_call(True, l, x, mod3, nw2, [attn, osum, gz], onw, bd512, w_even_out, w_ffn_in, w_ffn_out)
        else:
            wi = w_odd_in[j].astype(BF16)
            w_arr = jnp.concatenate([wi[:, :3072], _pad_cols(wi[:, 3072:3104], 0, LANES)], axis=1)
            q, k, v, og, gates = _odd_in_call(x, mod3, nw1, w_arr, _lane_row(mlstm_gate_bias[j]))
            hsum, c_dirs, m_dirs = None, [], []
            for d, (lay, tri_d) in enumerate(((lay_f, trib_f), (lay_b, trib_b))):
                c0 = state_mlstm_C[:, j, d]
                n0 = state_mlstm_n[:, j, d]
                cx0 = jnp.concatenate([c0, jnp.broadcast_to(n0[..., None], c0.shape)], axis=-1)
                cx0 = jnp.concatenate([jnp.zeros((BATCH, 4, LANES, 2 * LANES), F32),
                                       cx0.reshape(DEC_BATCH, 4, LANES, 2 * LANES)], axis=0)
                m0 = jnp.concatenate([jnp.zeros((BATCH, 8, LANES), F32),
                                      jnp.broadcast_to(state_mlstm_m[:, j, d][..., None], (DEC_BATCH, 8, LANES))], axis=0)
                hsum, cfin, mfin = _mlstm_scan_call(q, k, v, gates, cx0, m0, _mlstm_sel(d), tri_d, hsum, lay=lay)
                c_dirs.append(cfin[:BATCH].reshape(BATCH, 8, HALF, 2 * LANES))
                m_dirs.append(mfin[:BATCH, :, 0])
            cst = jnp.stack(c_dirs, axis=1)
            mc_new.append(cst[..., :LANES])
            mn_new.append(cst[..., LANES])
            mm_new.append(jnp.stack(m_dirs, axis=1))
            x = _ffn_call(False, l, x, mod3, nw2, [hsum, og], mlstm_out_norm[j].reshape(1, -1), None,
                          w_odd_out, w_ffn_in, w_ffn_out)

    return (x[:NP_ROWS].reshape(BATCH, SEQ, D_MODEL), x[NP_ROWS:].reshape(DEC_BATCH, DEC_SEQ, D_MODEL),
            jnp.stack(ckv_new, axis=1), jnp.stack(krope_new, axis=1), jnp.stack(gdn_new, axis=1),
            jnp.stack(mc_new, axis=1), jnp.stack(mn_new, axis=1), jnp.stack(mm_new, axis=1))
```
